```python
import jax
import jax.numpy as jnp
from jax import lax
import numpy as np

D_MODEL = 1024
BATCH = 32
SEQ = 2048
DEPTH = 1

CHUNK = 64
N_META = 16
Q_BLOCK = 128
ROPE_THETA = 10000.0
NORM_EPS = 1e-6
NEG_INF = -1e30

FOX_HEADS = 8
FOX_HEAD_DIM = 64
DSA_HEADS = 8
Q_RANK = 256
KV_RANK = 128
QK_NOPE_DIM = 64
QK_ROPE_DIM = 32
V_HEAD_DIM = 64
IDX_HEADS = 4
IDX_HEAD_DIM = 64
IDX_ROPE_DIM = 32
IDX_TOPK_MAX = 256
N_EXPERTS = 32
TOP_K = 4
D_EXPERT = D_MODEL
SWIGLU_ALPHA = 1.702
SWIGLU_LIMIT = 7.0
EXPERT_BLOCK = 128

FOX_WIDTH = FOX_HEADS * FOX_HEAD_DIM
DSA_WIDTH = DSA_HEADS * V_HEAD_DIM
IN_SPLITS = (FOX_WIDTH, FOX_WIDTH, FOX_WIDTH, FOX_HEADS, Q_RANK, KV_RANK, QK_ROPE_DIM, IDX_HEAD_DIM, IDX_HEADS, D_MODEL, D_MODEL)
D_IN = sum(IN_SPLITS)

kernel_name = 'hybrid_fox_dsa_moe_meta_layer'


def rms_norm(x, g):
    xf = x.astype(jnp.float32)
    y = xf * lax.rsqrt(jnp.mean(xf * xf, axis=-1, keepdims=True) + NORM_EPS)
    return (y * g.astype(jnp.float32)).astype(x.dtype)


def layer_norm(x, g, b):
    xf = x.astype(jnp.float32)
    mu = jnp.mean(xf, axis=-1, keepdims=True)
    xc = xf - mu
    var = jnp.mean(xc * xc, axis=-1, keepdims=True)
    return (xc * lax.rsqrt(var + NORM_EPS) * g.astype(jnp.float32) + b.astype(jnp.float32)).astype(x.dtype)


def rope(x, pos):
    d = x.shape[-1]
    half = d // 2
    inv = ROPE_THETA ** (-jnp.arange(half, dtype=jnp.float32) / half)
    ang = pos.astype(jnp.float32)[:, None] * inv[None, :]
    ang = ang.reshape(ang.shape[:1] + (1,) * (x.ndim - 3) + (half,))
    cos, sin = jnp.cos(ang), jnp.sin(ang)
    x1 = x[..., :half].astype(jnp.float32)
    x2 = x[..., half:].astype(jnp.float32)
    return jnp.concatenate([x1 * cos - x2 * sin, x1 * sin + x2 * cos], axis=-1).astype(x.dtype)


def rope_partial(x, pos, n_rot):
    return jnp.concatenate([rope(x[..., :n_rot], pos), x[..., n_rot:]], axis=-1)


def chunk_ids(pos):
    return jnp.where(pos < N_META, 0, 1 + (pos - N_META) // CHUNK)


def chunk_end(last):
    if last < N_META:
        return N_META
    return N_META + CHUNK * (1 + (last - N_META) // CHUNK)


def fox_attention(q, k, v, log_f_cum, pos, n_valid):
    Tp, Dh = q.shape[1], q.shape[-1]
    scale = Dh ** -0.5
    outs = []
    for start in range(0, Tp, Q_BLOCK):
        end = start + Q_BLOCK
        s = jnp.einsum('bqhd,bkhd->bhqk', q[:, start:end], k[:, :end]).astype(jnp.float32) * scale
        f_q = jnp.transpose(log_f_cum[:, start:end], (0, 2, 1))[..., None]
        f_k = jnp.transpose(log_f_cum[:, :end], (0, 2, 1))[:, :, None, :]
        qpos, kpos = pos[start:end], pos[:end]
        allowed = (kpos[None, :] <= qpos[:, None]) & (kpos[None, :] < n_valid)
        s = jnp.where(allowed, s + f_q - f_k, NEG_INF)
        p = jax.nn.softmax(s, axis=-1).astype(v.dtype)
        outs.append(jnp.einsum('bhqk,bkhd->bqhd', p, v[:, :end]))
    return jnp.concatenate(outs, axis=1)


def dsa_attention(q_lat, q_rope, c_kv, k_rope, q_idx, k_idx, idx_w, pos, n_valid, k_sel):
    Tp = q_lat.shape[1]
    chunk = chunk_ids(pos)
    scale = (QK_NOPE_DIM + QK_ROPE_DIM) ** -0.5
    gather = jax.vmap(lambda a, i: a[i])
    outs = []
    for start in range(0, Tp, Q_BLOCK):
        end = start + Q_BLOCK
        kend = min(Tp, max(chunk_end(end - 1), k_sel))
        dots = jnp.einsum('bqhd,bkd->bhqk', q_idx[:, start:end], k_idx[:, :kend]).astype(jnp.float32)
        iscore = jnp.einsum('bqh,bhqk->bqk', idx_w[:, start:end].astype(jnp.float32), jax.nn.relu(dots)) * (IDX_HEAD_DIM ** -0.5)
        admissible = (chunk[:kend][None, :] <= chunk[start:end][:, None]) & (pos[:kend][None, :] < n_valid)
        iscore = jnp.where(admissible[None], iscore, NEG_INF)
        top_score, top_idx = lax.top_k(iscore, k_sel)
        sel_ok = top_score > 0.5 * NEG_INF
        kv_sel = gather(c_kv, top_idx)
        kr_sel = gather(k_rope, top_idx)
        s = (jnp.einsum('bqhr,bqkr->bhqk', q_lat[:, start:end], kv_sel)
             + jnp.einsum('bqhd,bqkd->bhqk', q_rope[:, start:end], kr_sel)).astype(jnp.float32) * scale
        s = jnp.where(sel_ok[:, None], s, NEG_INF)
        p = jax.nn.softmax(s, axis=-1).astype(kv_sel.dtype)
        outs.append(jnp.einsum('bhqk,bqkr->bqhr', p, kv_sel))
    return jnp.concatenate(outs, axis=1)


def hybrid_mixer(xn, pos, n_valid, k_sel, w_in, b_forget, q_norm_g, w_uq, kv_norm_g, w_uk, w_uv,
                 w_qidx, kidx_ln_g, kidx_ln_b, w_branch_a, w_branch_b, w_out):
    B, Tp, _ = xn.shape
    proj = xn @ w_in
    offs = [int(o) for o in np.cumsum(IN_SPLITS)[:-1]]
    qa, ka, va, fa, cq, ckv, kr, kidx, iw, ga, gb = jnp.split(proj, offs, axis=-1)

    qa = qa.reshape(B, Tp, FOX_HEADS, FOX_HEAD_DIM)
    ka = ka.reshape(B, Tp, FOX_HEADS, FOX_HEAD_DIM)
    va = va.reshape(B, Tp, FOX_HEADS, FOX_HEAD_DIM)
    log_f = jax.nn.log_sigmoid((fa + b_forget).astype(jnp.float32))
    log_f_cum = jnp.cumsum(log_f, axis=1)
    o_a = fox_attention(qa, ka, va, log_f_cum, pos, n_valid).reshape(B, Tp, FOX_WIDTH)

    cq_n = rms_norm(cq, q_norm_g)
    q = (cq_n @ w_uq).reshape(B, Tp, DSA_HEADS, QK_NOPE_DIM + QK_ROPE_DIM)
    q_nope, q_rope = q[..., :QK_NOPE_DIM], rope(q[..., QK_NOPE_DIM:], pos)
    q_lat = jnp.einsum('bthn,hnr->bthr', q_nope, w_uk)
    c_kv = rms_norm(ckv, kv_norm_g)
    k_rope = rope(kr, pos)
    q_idx = rope_partial((cq_n @ w_qidx).reshape(B, Tp, IDX_HEADS, IDX_HEAD_DIM), pos, IDX_ROPE_DIM)
    k_idx = rope_partial(layer_norm(kidx, kidx_ln_g, kidx_ln_b), pos, IDX_ROPE_DIM)
    idx_w = iw * (IDX_HEADS ** -0.5)
    o_lat = dsa_attention(q_lat, q_rope, c_kv, k_rope, q_idx, k_idx, idx_w, pos, n_valid, k_sel)
    o_b = jnp.einsum('bthr,hrv->bthv', o_lat, w_uv).reshape(B, Tp, DSA_WIDTH)

    merged = jax.nn.sigmoid(ga) * (o_a @ w_branch_a) + jax.nn.sigmoid(gb) * (o_b @ w_branch_b)
    return merged @ w_out


def moe_ffn(h, w_router, b_router, w_gate_up, b_gate_up, w_down, b_down):
    N, D = h.shape
    logits = (h @ w_router + b_router).astype(jnp.float32)
    top_val, top_idx = lax.top_k(logits, TOP_K)
    gate = jax.nn.softmax(top_val, axis=-1)
    flat_e = top_idx.reshape(-1)
    flat_tok = jnp.repeat(jnp.arange(N, dtype=jnp.int32), TOP_K)
    flat_gate = gate.reshape(-1)
    order = jnp.argsort(flat_e, stable=True)
    se = flat_e[order]
    counts = jnp.bincount(flat_e, length=N_EXPERTS)
    padded = (counts + EXPERT_BLOCK - 1) // EXPERT_BLOCK * EXPERT_BLOCK
    start = jnp.cumsum(counts) - counts
    pstart = jnp.cumsum(padded) - padded
    dest = pstart[se] + jnp.arange(N * TOP_K, dtype=jnp.int32) - start[se]
    n_slots = (-(-(N * TOP_K) // EXPERT_BLOCK) + N_EXPERTS) * EXPERT_BLOCK
    n_blocks = n_slots // EXPERT_BLOCK
    slot_tok = jnp.full((n_slots,), N, jnp.int32).at[dest].set(flat_tok[order])
    slot_gate = jnp.zeros((n_slots,), jnp.float32).at[dest].set(flat_gate[order])
    block_start = jnp.arange(n_blocks, dtype=jnp.int32) * EXPERT_BLOCK
    block_expert = jnp.minimum(jnp.searchsorted(jnp.cumsum(padded), block_start, side='right'), N_EXPERTS - 1)
    h_pad = jnp.concatenate([h, jnp.zeros((1, D), h.dtype)], axis=0)

    def expert_block(args):
        tok, e, g = args
        xb = h_pad[tok]
        gu = xb @ w_gate_up[e] + b_gate_up[e]
        x_glu, x_lin = gu[:, :D_EXPERT], gu[:, D_EXPERT:]
        x_glu = jnp.minimum(x_glu, SWIGLU_LIMIT)
        x_lin = jnp.clip(x_lin, -SWIGLU_LIMIT, SWIGLU_LIMIT)
        act = x_glu * jax.nn.sigmoid(SWIGLU_ALPHA * x_glu) * (x_lin + 1)
        yb = act @ w_down[e] + b_down[e]
        return yb * g[:, None].astype(yb.dtype)

    y = lax.map(expert_block, (slot_tok.reshape(n_blocks, EXPERT_BLOCK), block_expert,
                               slot_gate.reshape(n_blocks, EXPERT_BLOCK)))
    out = jnp.zeros((N + 1, D), y.dtype).at[slot_tok].add(y.reshape(n_slots, D))
    return out[:N].astype(h.dtype)


def setup_inputs(seed: int = 0) -> dict:
    key = jax.random.key(seed)
    ks = jax.random.split(key, 24)
    f32 = jnp.float32
    L = DEPTH

    def nrm(k, shape, scale):
        return jax.random.normal(k, shape, f32) * scale

    def gain(k, shape):
        return 1.0 + 0.02 * jax.random.normal(k, shape, f32)

    return {
        'x': nrm(ks[0], (BATCH, SEQ, D_MODEL), 1.0),
        'meta_tokens': nrm(ks[1], (N_META, D_MODEL), 1.0),
        'norm_mix_g': gain(ks[2], (L, D_MODEL)),
        'w_in': nrm(ks[3], (L, D_MODEL, D_IN), D_MODEL ** -0.5),
        'b_forget': nrm(ks[4], (L, FOX_HEADS), 0.1),
        'q_norm_g': gain(ks[5], (L, Q_RANK)),
        'w_uq': nrm(ks[6], (L, Q_RANK, DSA_HEADS * (QK_NOPE_DIM + QK_ROPE_DIM)), Q_RANK ** -0.5),
        'kv_norm_g': gain(ks[7], (L, KV_RANK)),
        'w_uk': nrm(ks[8], (L, DSA_HEADS, QK_NOPE_DIM, KV_RANK), KV_RANK ** -0.5),
        'w_uv': nrm(ks[9], (L, DSA_HEADS, KV_RANK, V_HEAD_DIM), KV_RANK ** -0.5),
        'w_qidx': nrm(ks[10], (L, Q_RANK, IDX_HEADS * IDX_HEAD_DIM), Q_RANK ** -0.5),
        'kidx_ln_g': gain(ks[11], (L, IDX_HEAD_DIM)),
        'kidx_ln_b': nrm(ks[12], (L, IDX_HEAD_DIM), 0.02),
        'w_branch_a': nrm(ks[13], (L, FOX_WIDTH, D_MODEL), FOX_WIDTH ** -0.5),
        'w_branch_b': nrm(ks[14], (L, DSA_WIDTH, D_MODEL), DSA_WIDTH ** -0.5),
        'w_out': nrm(ks[15], (L, D_MODEL, D_MODEL), D_MODEL ** -0.5),
        'norm_ffn_g': gain(ks[16], (L, D_MODEL)),
        'w_router': nrm(ks[17], (L, D_MODEL, N_EXPERTS), D_MODEL ** -0.5),
        'b_router': nrm(ks[18], (L, N_EXPERTS), 0.01),
        'w_gate_up': nrm(ks[19], (L, N_EXPERTS, D_MODEL, 2 * D_EXPERT), D_MODEL ** -0.5),
        'b_gate_up': nrm(ks[20], (L, N_EXPERTS, 2 * D_EXPERT), 0.01),
        'w_down': nrm(ks[21], (L, N_EXPERTS, D_EXPERT, D_MODEL), D_EXPERT ** -0.5),
        'b_down': nrm(ks[22], (L, N_EXPERTS, D_MODEL), 0.01),
        'norm_final_g': gain(ks[23], (D_MODEL,)),
    }


def reference(x, meta_tokens, norm_mix_g, w_in, b_forget, q_norm_g, w_uq, kv_norm_g, w_uk, w_uv,
              w_qidx, kidx_ln_g, kidx_ln_b, w_branch_a, w_branch_b, w_out, norm_ffn_g, w_router,
              b_router, w_gate_up, b_gate_up, w_down, b_down, norm_final_g):
    B, S, D = x.shape
    T = S + N_META
    Tp = -(-T // Q_BLOCK) * Q_BLOCK
    k_sel = min(IDX_TOPK_MAX, S // 4)
    pos = jnp.arange(Tp, dtype=jnp.int32)
    h = jnp.concatenate([jnp.broadcast_to(meta_tokens.astype(x.dtype)[None], (B, N_META, D)), x], axis=1)
    for l in range(DEPTH):
        xn = jnp.pad(rms_norm(h, norm_mix_g[l]), ((0, 0), (0, Tp - T), (0, 0)))
        mix = hybrid_mixer(xn, pos, T, k_sel, w_in[l], b_forget[l], q_norm_g[l], w_uq[l], kv_norm_g[l],
                           w_uk[l], w_uv[l], w_qidx[l], kidx_ln_g[l], kidx_ln_b[l], w_branch_a[l],
                           w_branch_b[l], w_out[l])
        h = h + mix[:, :T]
        hn = rms_norm(h, norm_ffn_g[l]).reshape(B * T, D)
        h = h + moe_ffn(hn, w_router[l], b_router[l], w_gate_up[l], b_gate_up[l], w_down[l], b_down[l]).reshape(B, T, D)
    return rms_norm(h, norm_final_g)[:, N_META:]
```

```python
import functools

import numpy as np
import jax
import jax.numpy as jnp
from jax import lax
from jax.experimental import pallas as pl
from jax.experimental.pallas import tpu as pltpu

F32 = jnp.float32
BF16 = jnp.bfloat16
I32 = jnp.int32

D_MODEL = 1024
CHUNK = 64
N_META = 16
Q_BLOCK = 128
ROPE_THETA = 10000.0
NORM_EPS = 1e-6
NEG_INF = -1e30

FOX_HEADS = 8
FOX_HEAD_DIM = 64
DSA_HEADS = 8
Q_RANK = 256
KV_RANK = 128
QK_NOPE_DIM = 64
QK_ROPE_DIM = 32
V_HEAD_DIM = 64
IDX_HEADS = 4
IDX_HEAD_DIM = 64
IDX_ROPE_DIM = 32
IDX_TOPK_MAX = 256
N_EXPERTS = 32
TOP_K = 4
D_EXPERT = D_MODEL
SWIGLU_ALPHA = 1.702
SWIGLU_LIMIT = 7.0

FOX_WIDTH = FOX_HEADS * FOX_HEAD_DIM
DSA_WIDTH = DSA_HEADS * V_HEAD_DIM
IN_SPLITS = (FOX_WIDTH, FOX_WIDTH, FOX_WIDTH, FOX_HEADS, Q_RANK, KV_RANK, QK_ROPE_DIM,
             IDX_HEAD_DIM, IDX_HEADS, D_MODEL, D_MODEL)

LANES = 128
VMEM_LIMIT = 56 * 1024 * 1024
KEY_GROUP = 512
EXPERT_ROWS = 256

KR_LO, KIDX_LO = 0, 32
FA_LO, IW_LO = 64, 72

C_Q, C_K, C_V = 0, 512, 1024
C_CQ = 1536
C_G1, C_G2, C_G3 = 1792, 1920, 2048
C_GA, C_GB = 2176, 3200
C_TOTAL = 4224


def _cparams(sem):
    return pltpu.CompilerParams(dimension_semantics=sem, vmem_limit_bytes=VMEM_LIMIT)


def _row_tile(rows, cap):
    t = cap
    while rows % t:
        t //= 2
    return t


def _rot_cols(w):
    h = w.shape[-1] // 2
    return jnp.concatenate([-w[..., h:], w[..., :h]], axis=-1)


def _perm_cols(w):
    h = w.shape[-1] // 2
    return jnp.concatenate([w[..., h:], w[..., :h]], axis=-1)


def _layout_w_in(w_in):
    offs = np.cumsum((0,) + IN_SPLITS)
    qa, ka, va, fa, cq, ckv, kr, kidx, iw, ga, gb = [w_in[:, offs[i]:offs[i + 1]] for i in range(11)]
    d = w_in.shape[0]
    z = lambda n: jnp.zeros((d, n), w_in.dtype)
    g2 = jnp.concatenate([kr, kidx, z(32)], axis=1)
    g3 = jnp.concatenate([_rot_cols(kr), _perm_cols(kidx[:, :IDX_ROPE_DIM]), fa, iw, z(52)], axis=1)
    w = jnp.concatenate([qa, ka, va, cq, ckv, g2, g3, ga, gb], axis=1)
    assert w.shape[1] == C_TOTAL
    return w.astype(BF16)


def _layout_w_q(w_uq, w_qidx):
    r = w_uq.shape[0]
    wq = w_uq.reshape(r, DSA_HEADS, QK_NOPE_DIM + QK_ROPE_DIM)
    nope = wq[:, :, :QK_NOPE_DIM].reshape(r, DSA_HEADS * QK_NOPE_DIM)
    rope = wq[:, :, QK_NOPE_DIM:]
    pad_r = lambda a: jnp.pad(a, ((0, 0), (0, 0), (KR_LO, LANES - KR_LO - QK_ROPE_DIM))).reshape(r, -1)
    wi = w_qidx.reshape(r, IDX_HEADS, IDX_HEAD_DIM)
    wi_rot = jnp.concatenate([_rot_cols(wi[:, :, :IDX_ROPE_DIM]), jnp.zeros_like(wi[:, :, IDX_ROPE_DIM:])], axis=-1)
    pad_i = lambda a: jnp.pad(a, ((0, 0), (0, 0), (KIDX_LO, LANES - KIDX_LO - IDX_HEAD_DIM))).reshape(r, -1)
    w = jnp.concatenate([nope, pad_r(rope), pad_r(_rot_cols(rope)), pad_i(wi), pad_i(wi_rot)], axis=1)
    return w.astype(BF16)


def _rope_tables(tp):
    half = QK_ROPE_DIM // 2
    inv = ROPE_THETA ** (-jnp.arange(half, dtype=F32) / half)
    ang = jnp.arange(tp, dtype=F32)[:, None] * inv[None, :]
    cos, sin = jnp.cos(ang), jnp.sin(ang)
    cos32 = jnp.concatenate([cos, cos], axis=1)
    sin32 = jnp.concatenate([sin, sin], axis=1)
    one, zero = jnp.ones((tp, 1), F32), jnp.zeros((tp, 1), F32)
    cos_k = jnp.concatenate([cos32, cos32, jnp.tile(one, (1, 64))], axis=1)
    sin_k = jnp.concatenate([sin32, sin32, jnp.tile(zero, (1, 64))], axis=1)
    return cos_k, sin_k


def _inproj_kernel(x_ref, g_ref, w_ref, q_ref, k_ref, v_ref, cq_ref, sm_ref, ga_ref, gb_ref):
    x = x_ref[...]
    ms = jnp.mean(x * x, axis=-1, keepdims=True)
    xn = (x * lax.rsqrt(ms + NORM_EPS) * g_ref[...]).astype(BF16)

    def mm(lo, hi):
        return jnp.dot(xn, w_ref[:, lo:hi], preferred_element_type=F32)

    q_ref[...] = mm(C_Q, C_K).astype(BF16)
    k_ref[...] = mm(C_K, C_V).astype(BF16)
    v_ref[...] = mm(C_V, C_CQ).astype(BF16)
    cq_ref[...] = mm(C_CQ, C_G1)
    sm_ref[...] = mm(C_G1, C_GA)
    ga_ref[...] = (1.0 / (1.0 + jnp.exp(-mm(C_GA, C_GB)))).astype(BF16)
    gb_ref[...] = (1.0 / (1.0 + jnp.exp(-mm(C_GB, C_TOTAL)))).astype(BF16)


def _in_proj(hp, g, w_cat):
    rows, d = hp.shape
    tm = _row_tile(rows, 512)
    row = lambda n: pl.BlockSpec((tm, n), lambda i: (i, 0))
    full = lambda a: pl.BlockSpec(a.shape, lambda i: (0,) * a.ndim)
    outs = [(FOX_WIDTH, BF16)] * 3 + [(Q_RANK, F32), (3 * LANES, F32), (D_MODEL, BF16), (D_MODEL, BF16)]
    return pl.pallas_call(
        _inproj_kernel,
        grid=(rows // tm,),
        in_specs=[row(d), full(g), full(w_cat)],
        out_specs=[row(n) for n, _ in outs],
        out_shape=[jax.ShapeDtypeStruct((rows, n), dt) for n, dt in outs],
        compiler_params=_cparams(("parallel",)),
        name="in_proj",
    )(hp, g, w_cat)


def _split3(x):
    hi = x.astype(BF16)
    r1 = x - hi.astype(F32)
    mid = r1.astype(BF16)
    lo = (r1 - mid.astype(F32)).astype(BF16)
    return hi, mid, lo


def _fcum_kernel(sm_ref, b_ref, tri_ref, o_ref):
    tp = o_ref.shape[1]
    tri = tri_ref[...]
    carry = jnp.zeros((1, LANES), F32)
    for j in range(tp // LANES):
        z = sm_ref[0, j * LANES:(j + 1) * LANES, :] + b_ref[...]
        ls = jnp.minimum(z, 0.0) - jnp.log(1.0 + jnp.exp(-jnp.abs(z)))
        c = carry
        for part in _split3(ls):
            c = c + jnp.dot(tri, part, preferred_element_type=F32)
        o_ref[0, j * LANES:(j + 1) * LANES, :] = c
        carry = c[LANES - 1:LANES, :]


def _forget_cumsum(small3, b_forget, batch, tp):
    g3 = small3.reshape(batch, tp, 3 * LANES)
    bias = jnp.zeros((1, LANES), F32).at[0, FA_LO:FA_LO + FOX_HEADS].set(b_forget.astype(F32))
    tri = jnp.tril(jnp.ones((LANES, LANES), F32)).astype(BF16)
    return pl.pallas_call(
        _fcum_kernel,
        grid=(batch,),
        in_specs=[pl.BlockSpec((1, tp, LANES), lambda b: (b, 0, 2)),
                  pl.BlockSpec((1, LANES), lambda b: (0, 0)),
                  pl.BlockSpec((LANES, LANES), lambda b: (0, 0))],
        out_specs=pl.BlockSpec((1, tp, LANES), lambda b: (b, 0, 0)),
        out_shape=jax.ShapeDtypeStruct((batch, tp, LANES), F32),
        compiler_params=_cparams(("parallel",)),
        name="fcum",
    )(g3, bias, tri)


def _key_widths(tp):
    ws = list(range(KEY_GROUP, tp, KEY_GROUP)) + [tp]
    return ws


def _for_key_width(qi, tp, body, lookahead=0):
    lo = 0
    for w in _key_widths(tp):
        hi = max(w // Q_BLOCK - lookahead, 0)
        if w == tp:
            cond = qi >= lo
        else:
            cond = (qi >= lo) & (qi < hi)
        pl.when(cond)(functools.partial(body, w))
        lo = hi


def _fox_kernel(q_ref, k_ref, v_ref, fc_ref, fr_ref, o_ref):
    pair = pl.program_id(1)
    qi = pl.program_id(2)
    tp = k_ref.shape[1]
    scale = FOX_HEAD_DIM ** -0.5

    def body(w):
        q = q_ref[0]
        k = k_ref[0, :w, :]
        v = v_ref[0, :w, :]
        lane = lax.broadcasted_iota(I32, (Q_BLOCK, LANES), 1)
        qpos = qi * Q_BLOCK + lax.broadcasted_iota(I32, (Q_BLOCK, w), 0)
        kpos = lax.broadcasted_iota(I32, (Q_BLOCK, w), 1)
        allowed = kpos <= qpos
        outs = []
        for hh in range(2):
            in_head = (lane >= hh * FOX_HEAD_DIM) & (lane < (hh + 1) * FOX_HEAD_DIM)
            qh = jnp.where(in_head, q, jnp.zeros_like(q))
            s = lax.dot_general(qh, k, (((1,), (1,)), ((), ())), preferred_element_type=F32) * scale
            fq = jnp.sum(jnp.where(lane == FA_LO + 2 * pair + hh, fc_ref[0], 0.0), axis=-1, keepdims=True)
            fk = fr_ref[0, 0, hh:hh + 1, :w]
            s = jnp.where(allowed, s + fq - fk, NEG_INF)
            m = jnp.max(s, axis=-1, keepdims=True)
            p = jnp.exp(s - m)
            l = jnp.sum(p, axis=-1, keepdims=True)
            o = jnp.dot(p.astype(BF16), v, preferred_element_type=F32) * (1.0 / l)
            outs.append(o)
        o_ref[0] = jnp.where(lane < FOX_HEAD_DIM, outs[0], outs[1]).astype(BF16)

    _for_key_width(qi, tp, body)


def _fox_attention(q, k, v, fc, fr):
    batch, tp, _ = q.shape
    nq = tp // Q_BLOCK
    pairs = FOX_HEADS // 2
    return pl.pallas_call(
        _fox_kernel,
        grid=(batch, pairs, nq),
        in_specs=[pl.BlockSpec((1, Q_BLOCK, LANES), lambda b, p, i: (b, i, p)),
                  pl.BlockSpec((1, tp, LANES), lambda b, p, i: (b, 0, p)),
                  pl.BlockSpec((1, tp, LANES), lambda b, p, i: (b, 0, p)),
                  pl.BlockSpec((1, Q_BLOCK, LANES), lambda b, p, i: (b, i, 0)),
                  pl.BlockSpec((1, 1, 2, tp), lambda b, p, i: (b, p, 0, 0))],
        out_specs=pl.BlockSpec((1, Q_BLOCK, LANES), lambda b, p, i: (b, i, p)),
        out_shape=jax.ShapeDtypeStruct((batch, tp, FOX_WIDTH), BF16),
        compiler_params=_cparams(("parallel", "parallel", "arbitrary")),
        name="fox",
    )(q, k, v, fc, fr)


def _dsa_prep_kernel(cq_ref, sm_ref, qg_ref, kvg_ref, lng_ref, lnb_ref, lngp_ref, lnbp_ref,
                     cosk_ref, sink_ref, wq_ref, wuk_ref, qq_ref, qidx_ref, kk_ref):
    cq = cq_ref[...]
    ms = jnp.mean(cq * cq, axis=-1, keepdims=True)
    cqn = (cq * lax.rsqrt(ms + NORM_EPS) * qg_ref[...]).astype(BF16)
    cos = cosk_ref[...]
    sin = sink_ref[...]

    def mm(lo, hi):
        return jnp.dot(cqn, wq_ref[:, lo:hi], preferred_element_type=F32)

    nope_w = DSA_HEADS * QK_NOPE_DIM
    hw = DSA_HEADS * LANES
    iw = IDX_HEADS * LANES
    q_nope = mm(0, nope_w).astype(BF16)
    for h in range(DSA_HEADS):
        base = nope_w + h * LANES
        q_rope = mm(base, base + LANES) * cos + mm(base + hw, base + hw + LANES) * sin
        p = h // 2
        q_lat = jnp.dot(q_nope[:, p * LANES:(p + 1) * LANES], wuk_ref[h], preferred_element_type=F32)
        qq_ref[:, h * 2 * LANES:h * 2 * LANES + LANES] = q_lat.astype(BF16)
        qq_ref[:, h * 2 * LANES + LANES:(h + 1) * 2 * LANES] = q_rope.astype(BF16)
    for h in range(IDX_HEADS):
        base = nope_w + 2 * hw + h * LANES
        qi = mm(base, base + LANES) * cos + mm(base + iw, base + iw + LANES) * sin
        qidx_ref[:, h * LANES:(h + 1) * LANES] = qi.astype(BF16)

    ckv = sm_ref[:, 0:LANES]
    ms = jnp.mean(ckv * ckv, axis=-1, keepdims=True)
    kk_ref[:, 0:LANES] = (ckv * lax.rsqrt(ms + NORM_EPS) * kvg_ref[...]).astype(BF16)
    g2 = sm_ref[:, LANES:2 * LANES]
    g3 = sm_ref[:, 2 * LANES:3 * LANES]
    lane = lax.broadcasted_iota(I32, g2.shape, 1)
    is_idx = (lane >= KIDX_LO) & (lane < KIDX_LO + IDX_HEAD_DIM)
    is_rope = lane < KIDX_LO
    xi = jnp.where(is_idx, g2, 0.0)
    mu = jnp.sum(xi, axis=-1, keepdims=True) * (1.0 / IDX_HEAD_DIM)
    xc = jnp.where(is_idx, g2 - mu, 0.0)
    var = jnp.sum(xc * xc, axis=-1, keepdims=True) * (1.0 / IDX_HEAD_DIM)
    rstd = lax.rsqrt(var + NORM_EPS)
    y = xc * rstd * lng_ref[...] + lnb_ref[...]
    yp = (g3 - mu) * rstd * lngp_ref[...] + lnbp_ref[...]
    a = jnp.where(is_rope, g2, y)
    b = jnp.where(is_rope, g3, jnp.where(lane < KIDX_LO + IDX_ROPE_DIM, yp, 0.0))
    kk_ref[:, LANES:2 * LANES] = (a * cos + b * sin).astype(BF16)


def _dsa_prep(cq, small3, tp, q_norm_g, kv_norm_g, ln_g, ln_b, w_q, w_uk_pair):
    rows = cq.shape[0]
    tm = tp // 4 if tp % 64 == 0 else tp
    per = tp // tm
    cos_k, sin_k = _rope_tables(tp)
    lane_vec = lambda v, lo: jnp.zeros((1, LANES), F32).at[0, lo:lo + v.shape[0]].set(v.astype(F32))
    half = IDX_ROPE_DIM // 2
    sign = jnp.concatenate([-jnp.ones((half,), F32), jnp.ones((half,), F32)])
    g_perm = jnp.concatenate([ln_g[half:IDX_ROPE_DIM], ln_g[:half]]).astype(F32) * sign
    b_perm = jnp.concatenate([ln_b[half:IDX_ROPE_DIM], ln_b[:half]]).astype(F32) * sign
    row = lambda n: pl.BlockSpec((tm, n), lambda i: (i, 0))
    tab = pl.BlockSpec((tm, LANES), lambda i: (i % per, 0))
    full = lambda a: pl.BlockSpec(a.shape, lambda i: (0,) * a.ndim)
    vecs = [q_norm_g.astype(F32)[None], kv_norm_g.astype(F32)[None], lane_vec(ln_g, KIDX_LO),
            lane_vec(ln_b, KIDX_LO), lane_vec(g_perm, KIDX_LO), lane_vec(b_perm, KIDX_LO)]
    outs = [(DSA_HEADS * 2 * LANES, BF16), (IDX_HEADS * LANES, BF16), (2 * LANES, BF16)]
    return pl.pallas_call(
        _dsa_prep_kernel,
        grid=(rows // tm,),
        in_specs=[row(Q_RANK), row(3 * LANES)] + [full(v) for v in vecs] + [tab, tab, full(w_q), full(w_uk_pair)],
        out_specs=[row(n) for n, _ in outs],
        out_shape=[jax.ShapeDtypeStruct((rows, n), dt) for n, dt in outs],
        compiler_params=_cparams(("parallel",)),
        name="dsa_prep",
    )(cq, small3, *vecs, cos_k, sin_k, w_q, w_uk_pair)


def _dsa_kernel(qq_ref, qidx_ref, sm_ref, kk_ref, tri_ref, wuv_ref, o_ref, key_ref, bias_ref, *,
                n_valid, k_sel):
    qi = pl.program_id(1)
    tp = kk_ref.shape[1]
    scale = (QK_NOPE_DIM + QK_ROPE_DIM) ** -0.5
    int_min = jnp.int32(-2 ** 31)

    def body(w):
        kx = kk_ref[0, :w, LANES:]
        acc = jnp.zeros((Q_BLOCK, w), F32)
        for h in range(IDX_HEADS):
            d = lax.dot_general(qidx_ref[0, :, h * LANES:(h + 1) * LANES], kx,
                                (((1,), (1,)), ((), ())), preferred_element_type=F32)
            wh = sm_ref[0, :, IW_LO + h:IW_LO + h + 1] * (IDX_HEADS ** -0.5)
            acc = acc + wh * jnp.maximum(d, 0.0)
        isc = acc * (IDX_HEAD_DIM ** -0.5)
        isc = jnp.where(isc == 0.0, 0.0, isc)
        qpos = qi * Q_BLOCK + lax.broadcasted_iota(I32, (Q_BLOCK, 1), 0)
        kend = (((qpos + (CHUNK - N_META)) // CHUNK) + 1) * CHUNK - (CHUNK - N_META)
        kend = jnp.minimum(kend, n_valid)
        kpos = lax.broadcasted_iota(I32, (Q_BLOCK, w), 1)
        adm = kpos < kend
        bits = pltpu.bitcast(isc, I32)
        key = bits ^ ((bits >> 31) & jnp.int32(0x7FFFFFFF))
        key = jnp.where(adm, key, int_min)
        key_ref[:, :w] = key
        ok_bias = jnp.where(adm & (isc > 0.5 * NEG_INF), 0.0, NEG_INF)

        def count_ge(c):
            return jnp.sum(jnp.where(key_ref[:, :w] >= c, 1.0, 0.0), axis=-1, keepdims=True)

        kf = jnp.float32(k_sel)
        t0 = jnp.where(count_ge(jnp.zeros((Q_BLOCK, 1), I32)) >= kf, jnp.int32(0), int_min)

        def bit_step(i, t):
            cand = t | jnp.left_shift(jnp.int32(1), 30 - i)
            return jnp.where(count_ge(cand) >= kf, cand, t)

        thr = lax.fori_loop(0, 31, bit_step, t0)

        gt = key > thr
        need = kf - jnp.sum(jnp.where(gt, 1.0, 0.0), axis=-1, keepdims=True)
        run = jnp.zeros((Q_BLOCK, 1), F32)
        tri = tri_ref[...]
        for j in range(w // LANES):
            sl = slice(j * LANES, (j + 1) * LANES)
            eq = key[:, sl] == thr
            e = jnp.where(eq, 1.0, 0.0)
            pre = jnp.dot(e.astype(BF16), tri, preferred_element_type=F32) + run
            take = gt[:, sl] | (eq & (pre <= need))
            bias_ref[:, sl] = jnp.where(take, ok_bias[:, sl], NEG_INF)
            run = pre[:, LANES - 1:LANES]

        bias = bias_ref[:, :w]
        kk = kk_ref[0, :w, :]
        kv = kk_ref[0, :w, :LANES]
        for p in range(DSA_HEADS // 2):
            lat = []
            for h in (2 * p, 2 * p + 1):
                qh = qq_ref[0, :, h * 2 * LANES:(h + 1) * 2 * LANES]
                s = lax.dot_general(qh, kk, (((1,), (1,)), ((), ())), preferred_element_type=F32) * scale + bias
                m = jnp.max(s, axis=-1, keepdims=True)
                pr = jnp.exp(s - m)
                l = jnp.sum(pr, axis=-1, keepdims=True)
                lat.append((jnp.dot(pr.astype(BF16), kv, preferred_element_type=F32) * (1.0 / l)).astype(BF16))
            o_lat = jnp.concatenate(lat, axis=1)
            o_ref[0, :, p * LANES:(p + 1) * LANES] = jnp.dot(
                o_lat, wuv_ref[p], preferred_element_type=F32).astype(BF16)

    _for_key_width(qi, tp, body, lookahead=1)


def _dsa_attention(qq, qidx, small3, kk, w_uv_pair, n_valid, k_sel):
    batch, tp, _ = kk.shape
    nq = tp // Q_BLOCK
    tri = jnp.triu(jnp.ones((LANES, LANES), F32)).astype(BF16)
    qblk = lambda n: pl.BlockSpec((1, Q_BLOCK, n), lambda b, i: (b, i, 0))
    return pl.pallas_call(
        functools.partial(_dsa_kernel, n_valid=n_valid, k_sel=k_sel),
        grid=(batch, nq),
        in_specs=[qblk(DSA_HEADS * 2 * LANES), qblk(IDX_HEADS * LANES),
                  pl.BlockSpec((1, Q_BLOCK, LANES), lambda b, i: (b, i, 2)),
                  pl.BlockSpec((1, tp, 2 * LANES), lambda b, i: (b, 0, 0)),
                  pl.BlockSpec((LANES, LANES), lambda b, i: (0, 0)),
                  pl.BlockSpec(w_uv_pair.shape, lambda b, i: (0, 0, 0))],
        out_specs=qblk(DSA_WIDTH),
        out_shape=jax.ShapeDtypeStruct((batch, tp, DSA_WIDTH), BF16),
        scratch_shapes=[pltpu.VMEM((Q_BLOCK, tp), I32), pltpu.VMEM((Q_BLOCK, tp), F32)],
        compiler_params=_cparams(("parallel", "arbitrary")),
        name="dsa",
    )(qq, qidx, small3, kk, tri, w_uv_pair)


def _merge_kernel(oa_ref, ob_ref, ga_ref, gb_ref, h_ref, wa_ref, wb_ref, wo_ref, g_ref, wr_ref, br_ref,
                  h1_ref, hn_ref, ti_ref, tg_ref):
    a = jnp.dot(oa_ref[0], wa_ref[...], preferred_element_type=F32)
    b = jnp.dot(ob_ref[0], wb_ref[...], preferred_element_type=F32)
    merged = ga_ref[0].astype(F32) * a + gb_ref[0].astype(F32) * b
    mix = jnp.dot(merged.astype(BF16), wo_ref[...], preferred_element_type=F32)
    h1 = h_ref[0] + mix
    h1_ref[0] = h1
    ms = jnp.mean(h1 * h1, axis=-1, keepdims=True)
    hn = h1 * lax.rsqrt(ms + NORM_EPS) * g_ref[...]
    hn_ref[0] = hn
    logits = jnp.dot(hn.astype(BF16), wr_ref[...], preferred_element_type=F32) + br_ref[...]
    lane = lax.broadcasted_iota(I32, logits.shape, 1)
    idx_out = jnp.zeros(logits.shape, I32)
    val_out = jnp.zeros(logits.shape, F32)
    vals = []
    for kk in range(TOP_K):
        m = jnp.max(logits, axis=-1, keepdims=True)
        idx = jnp.min(jnp.where(logits == m, lane, LANES), axis=-1, keepdims=True)
        logits = jnp.where(lane == idx, -jnp.inf, logits)
        idx_out = jnp.where(lane == kk, idx, idx_out)
        vals.append(m)
    es = [jnp.exp(v - vals[0]) for v in vals]
    den = es[0] + es[1] + es[2] + es[3]
    for kk in range(TOP_K):
        val_out = jnp.where(lane == kk, es[kk] / den, val_out)
    ti_ref[0] = idx_out
    tg_ref[0] = val_out


def _merge_router(o_a, o_b, sga, sgb, hp, t_valid, w_a, w_b, w_o, g_ffn, w_r, b_r):
    batch, tp, d = hp.shape
    tt = t_valid // 3 if t_valid % 48 == 0 else t_valid
    blk = lambda n: pl.BlockSpec((1, tt, n), lambda b, j: (b, j, 0))
    full = lambda a: pl.BlockSpec(a.shape, lambda b, j: (0,) * a.ndim)
    consts = [w_a, w_b, w_o, g_ffn, w_r, b_r]
    outs = [(d, F32), (d, F32), (LANES, I32), (LANES, F32)]
    return pl.pallas_call(
        _merge_kernel,
        grid=(batch, t_valid // tt),
        in_specs=[blk(FOX_WIDTH), blk(DSA_WIDTH), blk(d), blk(d), blk(d)] + [full(c) for c in consts],
        out_specs=[blk(n) for n, _ in outs],
        out_shape=[jax.ShapeDtypeStruct((batch, t_valid, n), dt) for n, dt in outs],
        compiler_params=_cparams(("parallel", "parallel")),
        name="merge_router",
    )(o_a, o_b, sga, sgb, hp, *consts)


def _experts_kernel(be_ref, tok_ref, tokn_ref, dst_ref, hn_hbm, wgu_ref, bgu_ref, wd_ref, bd_ref, y_hbm,
                    xbuf, ybuf, gsem, ssem):
    del be_ref
    i = pl.program_id(0)
    nb = pl.num_programs(0)
    rows = xbuf.shape[1]
    slot = i % 2

    def start_gather(idx_ref, s):
        def issue(r, c):
            t = idx_ref[0, 0, r]
            pltpu.make_async_copy(hn_hbm.at[pl.ds(t, 1)], xbuf.at[s, pl.ds(r, 1)], gsem.at[s]).start()
            return c
        lax.fori_loop(0, rows, issue, 0)

    def wait_gather(s):
        pltpu.make_async_copy(hn_hbm.at[pl.ds(0, rows)], xbuf.at[s], gsem.at[s]).wait()

    def start_scatter(s):
        def issue(r, c):
            t = dst_ref[0, 0, r]
            pltpu.make_async_copy(ybuf.at[s, pl.ds(r, 1)], y_hbm.at[pl.ds(t, 1)], ssem.at[s]).start()
            return c
        lax.fori_loop(0, rows, issue, 0)

    def wait_scatter(s):
        pltpu.make_async_copy(ybuf.at[s], y_hbm.at[pl.ds(0, rows)], ssem.at[s]).wait()

    @pl.when(i == 0)
    def _():
        start_gather(tok_ref, 0)

    @pl.when(i + 1 < nb)
    def _():
        start_gather(tokn_ref, 1 - slot)

    wait_gather(slot)
    x = xbuf[slot].astype(BF16)
    gu = jnp.dot(x, wgu_ref[0], preferred_element_type=F32) + bgu_ref[0]
    glu = jnp.minimum(gu[:, :D_EXPERT], SWIGLU_LIMIT)
    lin = jnp.clip(gu[:, D_EXPERT:], -SWIGLU_LIMIT, SWIGLU_LIMIT)
    act = glu * (1.0 / (1.0 + jnp.exp(-SWIGLU_ALPHA * glu))) * (lin + 1.0)
    y = jnp.dot(act.astype(BF16), wd_ref[0], preferred_element_type=F32) + bd_ref[0]

    @pl.when(i >= 2)
    def _():
        wait_scatter(slot)

    ybuf[slot] = y
    start_scatter(slot)

    @pl.when(i == nb - 1)
    def _():
        wait_scatter(slot)

        @pl.when(nb > 1)
        def _():
            wait_scatter(1 - slot)


def _experts(hn, block_expert, slot_tok, slot_dst, w_gu, b_gu, w_d, b_d):
    n, d = hn.shape
    n_slots = slot_tok.shape[0]
    rows = EXPERT_ROWS
    nb = n_slots // rows
    tok3 = slot_tok.reshape(nb, 1, rows)
    dst3 = slot_dst.reshape(nb, 1, rows)
    smem = lambda f: pl.BlockSpec((1, 1, rows), f, memory_space=pltpu.SMEM)
    grid_spec = pltpu.PrefetchScalarGridSpec(
        num_scalar_prefetch=1,
        grid=(nb,),
        in_specs=[smem(lambda i, be: (i, 0, 0)),
                  smem(lambda i, be: (jnp.minimum(i + 1, nb - 1), 0, 0)),
                  smem(lambda i, be: (i, 0, 0)),
                  pl.BlockSpec(memory_space=pl.ANY),
                  pl.BlockSpec((1, d, 2 * D_EXPERT), lambda i, be: (be[i], 0, 0)),
                  pl.BlockSpec((1, 1, 2 * D_EXPERT), lambda i, be: (be[i], 0, 0)),
                  pl.BlockSpec((1, D_EXPERT, d), lambda i, be: (be[i], 0, 0)),
                  pl.BlockSpec((1, 1, d), lambda i, be: (be[i], 0, 0))],
        out_specs=pl.BlockSpec(memory_space=pl.ANY),
        scratch_shapes=[pltpu.VMEM((2, rows, d), F32), pltpu.VMEM((2, rows, d), F32),
                        pltpu.SemaphoreType.DMA((2,)), pltpu.SemaphoreType.DMA((2,))],
    )
    return pl.pallas_call(
        _experts_kernel,
        grid_spec=grid_spec,
        out_shape=jax.ShapeDtypeStruct((n_slots, d), F32),
        compiler_params=_cparams(("arbitrary",)),
        name="experts",
    )(block_expert, tok3, tok3, dst3, hn, w_gu, b_gu, w_d, b_d)


def _routing_tables(top_idx, n_tok):
    rows = EXPERT_ROWS
    n_asg = n_tok * TOP_K
    flat_e = top_idx.reshape(-1).astype(I32)
    keys = jnp.sort(flat_e * n_asg + jnp.arange(n_asg, dtype=I32))
    order = keys % n_asg
    counts = jnp.sum(flat_e[:, None] == jnp.arange(N_EXPERTS, dtype=I32)[None, :], axis=0).astype(I32)
    padded = (counts + rows - 1) // rows * rows
    start = jnp.cumsum(counts) - counts
    pend = jnp.cumsum(padded)
    pstart = pend - padded
    nb = -(-n_asg // rows) + N_EXPERTS
    n_slots = nb * rows
    block_expert = jnp.minimum(
        jnp.searchsorted(pend, jnp.arange(nb, dtype=I32) * rows, side='right'), N_EXPERTS - 1).astype(I32)
    slot = jnp.arange(n_slots, dtype=I32)
    e = jnp.repeat(block_expert, rows)
    r = slot - pstart[e]
    valid = (r >= 0) & (r < counts[e])
    asg = order[jnp.clip(start[e] + r, 0, n_asg - 1)]
    slot_tok = jnp.where(valid, asg // TOP_K, 0).astype(I32)
    pad_rank = jnp.cumsum(jnp.logical_not(valid).astype(I32)) - 1
    slot_dst = jnp.where(valid, asg, n_asg + pad_rank).astype(I32)
    return block_expert, slot_tok, slot_dst


def _combine_kernel(h_ref, y_ref, tg_ref, g_ref, o_ref, *, final_norm):
    d = h_ref.shape[-1]
    h = h_ref[0]
    for kk in range(TOP_K):
        h = h + y_ref[0, :, kk * d:(kk + 1) * d] * tg_ref[0, :, kk:kk + 1]
    if final_norm:
        ms = jnp.mean(h * h, axis=-1, keepdims=True)
        h = h * lax.rsqrt(ms + NORM_EPS) * g_ref[...]
    o_ref[0] = h


def _combine(h1, y4, tg, g, final_norm):
    batch, t, d = h1.shape
    tt = t // 6 if t % 48 == 0 else t
    blk = lambda n: pl.BlockSpec((1, tt, n), lambda b, j: (b, j, 0))
    return pl.pallas_call(
        functools.partial(_combine_kernel, final_norm=final_norm),
        grid=(batch, t // tt),
        in_specs=[blk(d), blk(TOP_K * d), blk(LANES), pl.BlockSpec((1, d), lambda b, j: (0, 0))],
        out_specs=blk(d),
        out_shape=jax.ShapeDtypeStruct((batch, t, d), F32),
        compiler_params=_cparams(("parallel", "parallel")),
        name="combine",
    )(h1, y4, tg, g)


def _pair_blockdiag(w):
    h, a, b = w.shape
    w = w.reshape(h // 2, 2, a, b)
    z = jnp.zeros_like(w[:, 0])
    top = jnp.concatenate([w[:, 0], z], axis=2)
    bot = jnp.concatenate([z, w[:, 1]], axis=2)
    return jnp.concatenate([top, bot], axis=1)


def _uk_per_head(w_uk):
    h, a, b = w_uk.shape
    z = jnp.zeros_like(w_uk)
    even = jnp.concatenate([w_uk, z], axis=1)
    odd = jnp.concatenate([z, w_uk], axis=1)
    sel = (jnp.arange(h) % 2 == 0)[:, None, None]
    return jnp.where(sel, even, odd)


def kernel(x, meta_tokens, norm_mix_g, w_in, b_forget, q_norm_g, w_uq, kv_norm_g, w_uk, w_uv, w_qidx,
           kidx_ln_g, kidx_ln_b, w_branch_a, w_branch_b, w_out, norm_ffn_g, w_router, b_router, w_gate_up,
           b_gate_up, w_down, b_down, norm_final_g):
    batch, seq, d = x.shape
    t = seq + N_META
    tp = -(-t // Q_BLOCK) * Q_BLOCK
    k_sel = min(IDX_TOPK_MAX, seq // 4)
    depth = w_in.shape[0]
    h = jnp.concatenate([jnp.broadcast_to(meta_tokens.astype(x.dtype)[None], (batch, N_META, d)), x], axis=1)
    for l in range(depth):
        hp = jnp.pad(h, ((0, 0), (0, tp - t), (0, 0)))
        rows = batch * tp
        q, k, v, cq, small3, sga, sgb = _in_proj(
            hp.reshape(rows, d), norm_mix_g[l].astype(F32)[None], _layout_w_in(w_in[l]))
        fc = _forget_cumsum(small3, b_forget[l], batch, tp)
        fr = jnp.transpose(fc[:, :, FA_LO:FA_LO + FOX_HEADS], (0, 2, 1)).reshape(batch, FOX_HEADS // 2, 2, tp)
        sh = lambda a: a.reshape(batch, tp, a.shape[-1])
        o_a = _fox_attention(sh(q), sh(k), sh(v), fc, fr)
        qq, qidx, kk = _dsa_prep(cq, small3, tp, q_norm_g[l], kv_norm_g[l], kidx_ln_g[l], kidx_ln_b[l],
                                 _layout_w_q(w_uq[l], w_qidx[l]), _uk_per_head(w_uk[l]).astype(BF16))
        o_b = _dsa_attention(sh(qq), sh(qidx), sh(small3), sh(kk), _pair_blockdiag(w_uv[l]).astype(BF16), t, k_sel)
        w_r = jnp.pad(w_router[l], ((0, 0), (0, LANES - N_EXPERTS))).astype(BF16)
        b_r = jnp.pad(b_router[l].astype(F32), (0, LANES - N_EXPERTS), constant_values=NEG_INF)[None]
        h1, hn, ti, tg = _merge_router(
            o_a, o_b, sh(sga), sh(sgb), hp, t, w_branch_a[l].astype(BF16), w_branch_b[l].astype(BF16),
            w_out[l].astype(BF16), norm_ffn_g[l].astype(F32)[None], w_r, b_r)
        n_tok = batch * t
        block_expert, slot_tok, slot_dst = _routing_tables(ti[:, :, :TOP_K], n_tok)
        y = _experts(hn.reshape(n_tok, d), block_expert, slot_tok, slot_dst,
                     w_gate_up[l].astype(BF16), b_gate_up[l].astype(F32)[:, None, :],
                     w_down[l].astype(BF16), b_down[l].astype(F32)[:, None, :])
        y4 = y[:n_tok * TOP_K].reshape(batch, t, TOP_K * d)
        h = _combine(h1, y4, tg, norm_final_g.astype(F32)[None], final_norm=(l == depth - 1))
    return h[:, N_META:]
```

```python
import functools

import numpy as np
import jax
import jax.numpy as jnp
from jax import lax
from jax.experimental import pallas as pl
from jax.experimental.pallas import tpu as pltpu

F32 = jnp.float32
BF16 = jnp.bfloat16
I32 = jnp.int32

D_MODEL = 1024
CHUNK = 64
N_META = 16
Q_BLOCK = 128
ROPE_THETA = 10000.0
NORM_EPS = 1e-6
NEG_INF = -1e30

FOX_HEADS = 8
FOX_HEAD_DIM = 64
DSA_HEADS = 8
Q_RANK = 256
KV_RANK = 128
QK_NOPE_DIM = 64
QK_ROPE_DIM = 32
V_HEAD_DIM = 64
IDX_HEADS = 4
IDX_HEAD_DIM = 64
IDX_ROPE_DIM = 32
IDX_TOPK_MAX = 256
N_EXPERTS = 32
TOP_K = 4
D_EXPERT = D_MODEL
SWIGLU_ALPHA = 1.702
SWIGLU_LIMIT = 7.0

FOX_WIDTH = FOX_HEADS * FOX_HEAD_DIM
DSA_WIDTH = DSA_HEADS * V_HEAD_DIM
IN_SPLITS = (FOX_WIDTH, FOX_WIDTH, FOX_WIDTH, FOX_HEADS, Q_RANK, KV_RANK, QK_ROPE_DIM,
             IDX_HEAD_DIM, IDX_HEADS, D_MODEL, D_MODEL)

LANES = 128
VMEM_LIMIT = 56 * 1024 * 1024
KEY_GROUP = 512
EXPERT_ROWS = 256

KR_LO, KIDX_LO = 0, 32
FA_LO, IW_LO = 64, 72

C_Q, C_K, C_V = 0, 512, 1024
C_CQ = 1536
C_G1, C_G2, C_G3 = 1792, 1920, 2048
C_GA, C_GB = 2176, 3200
C_TOTAL = 4224


def _cparams(sem):
    return pltpu.CompilerParams(dimension_semantics=sem, vmem_limit_bytes=VMEM_LIMIT)


def _row_tile(rows, cap):
    t = cap
    while rows % t:
        t //= 2
    return t


def _rot_cols(w):
    h = w.shape[-1] // 2
    return jnp.concatenate([-w[..., h:], w[..., :h]], axis=-1)


def _perm_cols(w):
    h = w.shape[-1] // 2
    return jnp.concatenate([w[..., h:], w[..., :h]], axis=-1)


def _layout_w_in(w_in):
    offs = np.cumsum((0,) + IN_SPLITS)
    qa, ka, va, fa, cq, ckv, kr, kidx, iw, ga, gb = [w_in[:, offs[i]:offs[i + 1]] for i in range(11)]
    d = w_in.shape[0]
    z = lambda n: jnp.zeros((d, n), w_in.dtype)
    g2 = jnp.concatenate([kr, kidx, z(32)], axis=1)
    g3 = jnp.concatenate([_rot_cols(kr), _perm_cols(kidx[:, :IDX_ROPE_DIM]), fa, iw, z(52)], axis=1)
    w = jnp.concatenate([qa, ka, va, cq, ckv, g2, g3, ga, gb], axis=1)
    assert w.shape[1] == C_TOTAL
    return w.astype(BF16)


def _layout_w_q(w_uq, w_qidx):
    r = w_uq.shape[0]
    wq = w_uq.reshape(r, DSA_HEADS, QK_NOPE_DIM + QK_ROPE_DIM)
    nope = wq[:, :, :QK_NOPE_DIM].reshape(r, DSA_HEADS * QK_NOPE_DIM)
    rope = wq[:, :, QK_NOPE_DIM:]
    pad_r = lambda a: jnp.pad(a, ((0, 0), (0, 0), (KR_LO, LANES - KR_LO - QK_ROPE_DIM))).reshape(r, -1)
    wi = w_qidx.reshape(r, IDX_HEADS, IDX_HEAD_DIM)
    wi_rot = jnp.concatenate([_rot_cols(wi[:, :, :IDX_ROPE_DIM]), jnp.zeros_like(wi[:, :, IDX_ROPE_DIM:])], axis=-1)
    pad_i = lambda a: jnp.pad(a, ((0, 0), (0, 0), (KIDX_LO, LANES - KIDX_LO - IDX_HEAD_DIM))).reshape(r, -1)
    w = jnp.concatenate([nope, pad_r(rope), pad_r(_rot_cols(rope)), pad_i(wi), pad_i(wi_rot)], axis=1)
    return w.astype(BF16)


def _rope_tables(tp):
    half = QK_ROPE_DIM // 2
    inv = ROPE_THETA ** (-jnp.arange(half, dtype=F32) / half)
    ang = jnp.arange(tp, dtype=F32)[:, None] * inv[None, :]
    cos, sin = jnp.cos(ang), jnp.sin(ang)
    cos32 = jnp.concatenate([cos, cos], axis=1)
    sin32 = jnp.concatenate([sin, sin], axis=1)
    one, zero = jnp.ones((tp, 1), F32), jnp.zeros((tp, 1), F32)
    cos_k = jnp.concatenate([cos32, cos32, jnp.tile(one, (1, 64))], axis=1)
    sin_k = jnp.concatenate([sin32, sin32, jnp.tile(zero, (1, 64))], axis=1)
    return cos_k, sin_k


def _inproj_kernel(x_ref, g_ref, w_ref, q_ref, k_ref, v_ref, cq_ref, sm_ref, ga_ref, gb_ref):
    x = x_ref[...]
    ms = jnp.mean(x * x, axis=-1, keepdims=True)
    xn = (x * lax.rsqrt(ms + NORM_EPS) * g_ref[...]).astype(BF16)

    def mm(lo, hi):
        return jnp.dot(xn, w_ref[:, lo:hi], preferred_element_type=F32)

    q_ref[...] = mm(C_Q, C_K).astype(BF16)
    k_ref[...] = mm(C_K, C_V).astype(BF16)
    v_ref[...] = mm(C_V, C_CQ).astype(BF16)
    cq_ref[...] = mm(C_CQ, C_G1)
    sm_ref[...] = mm(C_G1, C_GA)
    ga_ref[...] = (1.0 / (1.0 + jnp.exp(-mm(C_GA, C_GB)))).astype(BF16)
    gb_ref[...] = (1.0 / (1.0 + jnp.exp(-mm(C_GB, C_TOTAL)))).astype(BF16)


def _in_proj(hp, g, w_cat):
    rows, d = hp.shape
    tm = _row_tile(rows, 512)
    row = lambda n: pl.BlockSpec((tm, n), lambda i: (i, 0))
    full = lambda a: pl.BlockSpec(a.shape, lambda i: (0,) * a.ndim)
    outs = [(FOX_WIDTH, BF16)] * 3 + [(Q_RANK, F32), (3 * LANES, F32), (D_MODEL, BF16), (D_MODEL, BF16)]
    return pl.pallas_call(
        _inproj_kernel,
        grid=(rows // tm,),
        in_specs=[row(d), full(g), full(w_cat)],
        out_specs=[row(n) for n, _ in outs],
        out_shape=[jax.ShapeDtypeStruct((rows, n), dt) for n, dt in outs],
        compiler_params=_cparams(("parallel",)),
        name="in_proj",
    )(hp, g, w_cat)


def _split3(x):
    hi = x.astype(BF16)
    r1 = x - hi.astype(F32)
    mid = r1.astype(BF16)
    lo = (r1 - mid.astype(F32)).astype(BF16)
    return hi, mid, lo


def _fcum_kernel(sm_ref, b_ref, tri_ref, o_ref):
    tp = o_ref.shape[1]
    tri = tri_ref[...]
    carry = jnp.zeros((1, LANES), F32)
    for j in range(tp // LANES):
        z = sm_ref[0, j * LANES:(j + 1) * LANES, :] + b_ref[...]
        ls = jnp.minimum(z, 0.0) - jnp.log(1.0 + jnp.exp(-jnp.abs(z)))
        c = carry
        for part in _split3(ls):
            c = c + jnp.dot(tri, part, preferred_element_type=F32)
        o_ref[0, j * LANES:(j + 1) * LANES, :] = c
        carry = c[LANES - 1:LANES, :]


def _forget_cumsum(small3, b_forget, batch, tp):
    g3 = small3.reshape(batch, tp, 3 * LANES)
    bias = jnp.zeros((1, LANES), F32).at[0, FA_LO:FA_LO + FOX_HEADS].set(b_forget.astype(F32))
    tri = jnp.tril(jnp.ones((LANES, LANES), F32)).astype(BF16)
    return pl.pallas_call(
        _fcum_kernel,
        grid=(batch,),
        in_specs=[pl.BlockSpec((1, tp, LANES), lambda b: (b, 0, 2)),
                  pl.BlockSpec((1, LANES), lambda b: (0, 0)),
                  pl.BlockSpec((LANES, LANES), lambda b: (0, 0))],
        out_specs=pl.BlockSpec((1, tp, LANES), lambda b: (b, 0, 0)),
        out_shape=jax.ShapeDtypeStruct((batch, tp, LANES), F32),
        compiler_params=_cparams(("parallel",)),
        name="fcum",
    )(g3, bias, tri)


def _key_widths(tp):
    ws = list(range(KEY_GROUP, tp, KEY_GROUP)) + [tp]
    return ws


def _for_key_width(qi, tp, body, lookahead=0):
    lo = 0
    for w in _key_widths(tp):
        hi = max(w // Q_BLOCK - lookahead, 0)
        if w == tp:
            cond = qi >= lo
        else:
            cond = (qi >= lo) & (qi < hi)
        pl.when(cond)(functools.partial(body, w))
        lo = hi


def _fox_kernel(q_ref, k_ref, v_ref, fc_ref, fr_ref, o_ref):
    pair = pl.program_id(1)
    qi = pl.program_id(2)
    tp = k_ref.shape[1]
    scale = FOX_HEAD_DIM ** -0.5

    def body(w):
        q = q_ref[0]
        k = k_ref[0, :w, :]
        v = v_ref[0, :w, :]
        lane = lax.broadcasted_iota(I32, (Q_BLOCK, LANES), 1)
        qpos = qi * Q_BLOCK + lax.broadcasted_iota(I32, (Q_BLOCK, w), 0)
        kpos = lax.broadcasted_iota(I32, (Q_BLOCK, w), 1)
        allowed = kpos <= qpos
        outs = []
        for hh in range(2):
            in_head = (lane >= hh * FOX_HEAD_DIM) & (lane < (hh + 1) * FOX_HEAD_DIM)
            qh = jnp.where(in_head, q, jnp.zeros_like(q))
            s = lax.dot_general(qh, k, (((1,), (1,)), ((), ())), preferred_element_type=F32) * scale
            fq = jnp.sum(jnp.where(lane == FA_LO + 2 * pair + hh, fc_ref[0], 0.0), axis=-1, keepdims=True)
            fk = fr_ref[0, 0, hh:hh + 1, :w]
            s = jnp.where(allowed, s + fq - fk, NEG_INF)
            m = jnp.max(s, axis=-1, keepdims=True)
            p = jnp.exp(s - m)
            l = jnp.sum(p, axis=-1, keepdims=True)
            o = jnp.dot(p.astype(BF16), v, preferred_element_type=F32) * (1.0 / l)
            outs.append(o)
        o_ref[0] = jnp.where(lane < FOX_HEAD_DIM, outs[0], outs[1]).astype(BF16)

    _for_key_width(qi, tp, body)


def _fox_attention(q, k, v, fc, fr):
    batch, tp, _ = q.shape
    nq = tp // Q_BLOCK
    pairs = FOX_HEADS // 2
    return pl.pallas_call(
        _fox_kernel,
        grid=(batch, pairs, nq),
        in_specs=[pl.BlockSpec((1, Q_BLOCK, LANES), lambda b, p, i: (b, i, p)),
                  pl.BlockSpec((1, tp, LANES), lambda b, p, i: (b, 0, p)),
                  pl.BlockSpec((1, tp, LANES), lambda b, p, i: (b, 0, p)),
                  pl.BlockSpec((1, Q_BLOCK, LANES), lambda b, p, i: (b, i, 0)),
                  pl.BlockSpec((1, 1, 2, tp), lambda b, p, i: (b, p, 0, 0))],
        out_specs=pl.BlockSpec((1, Q_BLOCK, LANES), lambda b, p, i: (b, i, p)),
        out_shape=jax.ShapeDtypeStruct((batch, tp, FOX_WIDTH), BF16),
        compiler_params=_cparams(("parallel", "parallel", "arbitrary")),
        name="fox",
    )(q, k, v, fc, fr)


def _dsa_prep_kernel(cq_ref, sm_ref, qg_ref, kvg_ref, lng_ref, lnb_ref, lngp_ref, lnbp_ref,
                     cosk_ref, sink_ref, wq_ref, wuk_ref, qq_ref, qidx_ref, kk_ref):
    cq = cq_ref[...]
    ms = jnp.mean(cq * cq, axis=-1, keepdims=True)
    cqn = (cq * lax.rsqrt(ms + NORM_EPS) * qg_ref[...]).astype(BF16)
    cos = cosk_ref[...]
    sin = sink_ref[...]

    def mm(lo, hi):
        return jnp.dot(cqn, wq_ref[:, lo:hi], preferred_element_type=F32)

    nope_w = DSA_HEADS * QK_NOPE_DIM
    hw = DSA_HEADS * LANES
    iw = IDX_HEADS * LANES
    q_nope = mm(0, nope_w).astype(BF16)
    for h in range(DSA_HEADS):
        base = nope_w + h * LANES
        q_rope = mm(base, base + LANES) * cos + mm(base + hw, base + hw + LANES) * sin
        p = h // 2
        q_lat = jnp.dot(q_nope[:, p * LANES:(p + 1) * LANES], wuk_ref[h], preferred_element_type=F32)
        qq_ref[:, h * 2 * LANES:h * 2 * LANES + LANES] = q_lat.astype(BF16)
        qq_ref[:, h * 2 * LANES + LANES:(h + 1) * 2 * LANES] = q_rope.astype(BF16)
    for h in range(IDX_HEADS):
        base = nope_w + 2 * hw + h * LANES
        qi = mm(base, base + LANES) * cos + mm(base + iw, base + iw + LANES) * sin
        qidx_ref[:, h * LANES:(h + 1) * LANES] = qi.astype(BF16)

    ckv = sm_ref[:, 0:LANES]
    ms = jnp.mean(ckv * ckv, axis=-1, keepdims=True)
    kk_ref[:, 0:LANES] = (ckv * lax.rsqrt(ms + NORM_EPS) * kvg_ref[...]).astype(BF16)
    g2 = sm_ref[:, LANES:2 * LANES]
    g3 = sm_ref[:, 2 * LANES:3 * LANES]
    lane = lax.broadcasted_iota(I32, g2.shape, 1)
    is_idx = (lane >= KIDX_LO) & (lane < KIDX_LO + IDX_HEAD_DIM)
    is_rope = lane < KIDX_LO
    xi = jnp.where(is_idx, g2, 0.0)
    mu = jnp.sum(xi, axis=-1, keepdims=True) * (1.0 / IDX_HEAD_DIM)
    xc = jnp.where(is_idx, g2 - mu, 0.0)
    var = jnp.sum(xc * xc, axis=-1, keepdims=True) * (1.0 / IDX_HEAD_DIM)
    rstd = lax.rsqrt(var + NORM_EPS)
    y = xc * rstd * lng_ref[...] + lnb_ref[...]
    yp = (g3 - mu) * rstd * lngp_ref[...] + lnbp_ref[...]
    a = jnp.where(is_rope, g2, y)
    b = jnp.where(is_rope, g3, jnp.where(lane < KIDX_LO + IDX_ROPE_DIM, yp, 0.0))
    kk_ref[:, LANES:2 * LANES] = (a * cos + b * sin).astype(BF16)


def _dsa_prep(cq, small3, tp, q_norm_g, kv_norm_g, ln_g, ln_b, w_q, w_uk_pair):
    rows = cq.shape[0]
    tm = tp // 4 if tp % 64 == 0 else tp
    per = tp // tm
    cos_k, sin_k = _rope_tables(tp)
    lane_vec = lambda v, lo: jnp.zeros((1, LANES), F32).at[0, lo:lo + v.shape[0]].set(v.astype(F32))
    half = IDX_ROPE_DIM // 2
    sign = jnp.concatenate([-jnp.ones((half,), F32), jnp.ones((half,), F32)])
    g_perm = jnp.concatenate([ln_g[half:IDX_ROPE_DIM], ln_g[:half]]).astype(F32) * sign
    b_perm = jnp.concatenate([ln_b[half:IDX_ROPE_DIM], ln_b[:half]]).astype(F32) * sign
    row = lambda n: pl.BlockSpec((tm, n), lambda i: (i, 0))
    tab = pl.BlockSpec((tm, LANES), lambda i: (i % per, 0))
    full = lambda a: pl.BlockSpec(a.shape, lambda i: (0,) * a.ndim)
    vecs = [q_norm_g.astype(F32)[None], kv_norm_g.astype(F32)[None], lane_vec(ln_g, KIDX_LO),
            lane_vec(ln_b, KIDX_LO), lane_vec(g_perm, KIDX_LO), lane_vec(b_perm, KIDX_LO)]
    outs = [(DSA_HEADS * 2 * LANES, BF16), (IDX_HEADS * LANES, BF16), (2 * LANES, BF16)]
    return pl.pallas_call(
        _dsa_prep_kernel,
        grid=(rows // tm,),
        in_specs=[row(Q_RANK), row(3 * LANES)] + [full(v) for v in vecs] + [tab, tab, full(w_q), full(w_uk_pair)],
        out_specs=[row(n) for n, _ in outs],
        out_shape=[jax.ShapeDtypeStruct((rows, n), dt) for n, dt in outs],
        compiler_params=_cparams(("parallel",)),
        name="dsa_prep",
    )(cq, small3, *vecs, cos_k, sin_k, w_q, w_uk_pair)


def _dsa_kernel(qq_ref, qidx_ref, sm_ref, kk_ref, tri_ref, wuv_ref, o_ref, key_ref, bias_ref, *,
                n_valid, k_sel):
    qi = pl.program_id(1)
    tp = kk_ref.shape[1]
    scale = (QK_NOPE_DIM + QK_ROPE_DIM) ** -0.5
    int_min = jnp.int32(-2 ** 31)

    def body(w):
        kx = kk_ref[0, :w, LANES:]
        acc = jnp.zeros((Q_BLOCK, w), F32)
        for h in range(IDX_HEADS):
            d = lax.dot_general(qidx_ref[0, :, h * LANES:(h + 1) * LANES], kx,
                                (((1,), (1,)), ((), ())), preferred_element_type=F32)
            wh = sm_ref[0, :, IW_LO + h:IW_LO + h + 1] * (IDX_HEADS ** -0.5)
            acc = acc + wh * jnp.maximum(d, 0.0)
        isc = acc * (IDX_HEAD_DIM ** -0.5)
        isc = jnp.where(isc == 0.0, 0.0, isc)
        qpos = qi * Q_BLOCK + lax.broadcasted_iota(I32, (Q_BLOCK, 1), 0)
        kend = (((qpos + (CHUNK - N_META)) // CHUNK) + 1) * CHUNK - (CHUNK - N_META)
        kend = jnp.minimum(kend, n_valid)
        kpos = lax.broadcasted_iota(I32, (Q_BLOCK, w), 1)
        adm = kpos < kend
        bits = pltpu.bitcast(isc, I32)
        key = bits ^ ((bits >> 31) & jnp.int32(0x7FFFFFFF))
        key = jnp.where(adm, key, int_min)
        key_ref[:, :w] = key
        ok_bias = jnp.where(adm & (isc > 0.5 * NEG_INF), 0.0, NEG_INF)

        def count_ge(c):
            return jnp.sum(jnp.where(key_ref[:, :w] >= c, 1.0, 0.0), axis=-1, keepdims=True)

        kf = jnp.float32(k_sel)
        t0 = jnp.where(count_ge(jnp.zeros((Q_BLOCK, 1), I32)) >= kf, jnp.int32(0), int_min)

        def bit_step(i, t):
            cand = t | jnp.left_shift(jnp.int32(1), 30 - i)
            return jnp.where(count_ge(cand) >= kf, cand, t)

        thr = lax.fori_loop(0, 31, bit_step, t0)

        gt = key > thr
        need = kf - jnp.sum(jnp.where(gt, 1.0, 0.0), axis=-1, keepdims=True)
        run = jnp.zeros((Q_BLOCK, 1), F32)
        tri = tri_ref[...]
        for j in range(w // LANES):
            sl = slice(j * LANES, (j + 1) * LANES)
            eq = key[:, sl] == thr
            e = jnp.where(eq, 1.0, 0.0)
            pre = jnp.dot(e.astype(BF16), tri, preferred_element_type=F32) + run
            take = gt[:, sl] | (eq & (pre <= need))
            bias_ref[:, sl] = jnp.where(take, ok_bias[:, sl], NEG_INF)
            run = pre[:, LANES - 1:LANES]

        bias = bias_ref[:, :w]
        kk = kk_ref[0, :w, :]
        kv = kk_ref[0, :w, :LANES]
        for p in range(DSA_HEADS // 2):
            lat = []
            for h in (2 * p, 2 * p + 1):
                qh = qq_ref[0, :, h * 2 * LANES:(h + 1) * 2 * LANES]
                s = lax.dot_general(qh, kk, (((1,), (1,)), ((), ())), preferred_element_type=F32) * scale + bias
                m = jnp.max(s, axis=-1, keepdims=True)
                pr = jnp.exp(s - m)
                l = jnp.sum(pr, axis=-1, keepdims=True)
                lat.append((jnp.dot(pr.astype(BF16), kv, preferred_element_type=F32) * (1.0 / l)).astype(BF16))
            o_lat = jnp.concatenate(lat, axis=1)
            o_ref[0, :, p * LANES:(p + 1) * LANES] = jnp.dot(
                o_lat, wuv_ref[p], preferred_element_type=F32).astype(BF16)

    _for_key_width(qi, tp, body, lookahead=1)


def _dsa_attention(qq, qidx, small3, kk, w_uv_pair, n_valid, k_sel):
    batch, tp, _ = kk.shape
    nq = tp // Q_BLOCK
    tri = jnp.triu(jnp.ones((LANES, LANES), F32)).astype(BF16)
    qblk = lambda n: pl.BlockSpec((1, Q_BLOCK, n), lambda b, i: (b, i, 0))
    return pl.pallas_call(
        functools.partial(_dsa_kernel, n_valid=n_valid, k_sel=k_sel),
        grid=(batch, nq),
        in_specs=[qblk(DSA_HEADS * 2 * LANES), qblk(IDX_HEADS * LANES),
                  pl.BlockSpec((1, Q_BLOCK, LANES), lambda b, i: (b, i, 2)),
                  pl.BlockSpec((1, tp, 2 * LANES), lambda b, i: (b, 0, 0)),
                  pl.BlockSpec((LANES, LANES), lambda b, i: (0, 0)),
                  pl.BlockSpec(w_uv_pair.shape, lambda b, i: (0, 0, 0))],
        out_specs=qblk(DSA_WIDTH),
        out_shape=jax.ShapeDtypeStruct((batch, tp, DSA_WIDTH), BF16),
        scratch_shapes=[pltpu.VMEM((Q_BLOCK, tp), I32), pltpu.VMEM((Q_BLOCK, tp), F32)],
        compiler_params=_cparams(("parallel", "arbitrary")),
        name="dsa",
    )(qq, qidx, small3, kk, tri, w_uv_pair)


def _merge_kernel(oa_ref, ob_ref, ga_ref, gb_ref, h_ref, wa_ref, wb_ref, wo_ref, g_ref, wr_ref, br_ref,
                  h1_ref, hn_ref, ti_ref, tg_ref):
    a = jnp.dot(oa_ref[0], wa_ref[...], preferred_element_type=F32)
    b = jnp.dot(ob_ref[0], wb_ref[...], preferred_element_type=F32)
    merged = ga_ref[0].astype(F32) * a + gb_ref[0].astype(F32) * b
    mix = jnp.dot(merged.astype(BF16), wo_ref[...], preferred_element_type=F32)
    h1 = h_ref[0] + mix
    h1_ref[0] = h1
    ms = jnp.mean(h1 * h1, axis=-1, keepdims=True)
    hn = h1 * lax.rsqrt(ms + NORM_EPS) * g_ref[...]
    hn_ref[0] = hn
    logits = jnp.dot(hn.astype(BF16), wr_ref[...], preferred_element_type=F32) + br_ref[...]
    lane = lax.broadcasted_iota(I32, logits.shape, 1)
    idx_out = jnp.zeros(logits.shape, I32)
    val_out = jnp.zeros(logits.shape, F32)
    vals = []
    for kk in range(TOP_K):
        m = jnp.max(logits, axis=-1, keepdims=True)
        idx = jnp.min(jnp.where(logits == m, lane, LANES), axis=-1, keepdims=True)
        logits = jnp.where(lane == idx, -jnp.inf, logits)
        idx_out = jnp.where(lane == kk, idx, idx_out)
        vals.append(m)
    es = [jnp.exp(v - vals[0]) for v in vals]
    den = es[0] + es[1] + es[2] + es[3]
    for kk in range(TOP_K):
        val_out = jnp.where(lane == kk, es[kk] / den, val_out)
    ti_ref[0] = idx_out
    tg_ref[0] = val_out


def _merge_router(o_a, o_b, sga, sgb, hp, t_valid, w_a, w_b, w_o, g_ffn, w_r, b_r):
    batch, tp, d = hp.shape
    tt = t_valid // 3 if t_valid % 48 == 0 else t_valid
    blk = lambda n: pl.BlockSpec((1, tt, n), lambda b, j: (b, j, 0))
    full = lambda a: pl.BlockSpec(a.shape, lambda b, j: (0,) * a.ndim)
    consts = [w_a, w_b, w_o, g_ffn, w_r, b_r]
    outs = [(d, F32), (d, F32), (LANES, I32), (LANES, F32)]
    return pl.pallas_call(
        _merge_kernel,
        grid=(batch, t_valid // tt),
        in_specs=[blk(FOX_WIDTH), blk(DSA_WIDTH), blk(d), blk(d), blk(d)] + [full(c) for c in consts],
        out_specs=[blk(n) for n, _ in outs],
        out_shape=[jax.ShapeDtypeStruct((batch, t_valid, n), dt) for n, dt in outs],
        compiler_params=_cparams(("parallel", "parallel")),
        name="merge_router",
    )(o_a, o_b, sga, sgb, hp, *consts)


def _experts_kernel(be_ref, tok_ref, tokn_ref, dstp_ref, dst_ref, hn_hbm, wgu_ref, bgu_ref, wd_ref, bd_ref,
                    y_hbm, x0, x1, y0, y1, gsem, ssem):
    del be_ref
    i = pl.program_id(0)
    nb = pl.num_programs(0)
    rows = x0.shape[0]
    xbufs, ybufs = (x0, x1), (y0, y1)

    def gather_row(idx_ref, s, r):
        return pltpu.make_async_copy(hn_hbm.at[pl.ds(idx_ref[0, 0, r], 1)], xbufs[s].at[pl.ds(r, 1)], gsem.at[s])

    def scatter_row(idx_ref, s, r):
        return pltpu.make_async_copy(ybufs[s].at[pl.ds(r, 1)], y_hbm.at[pl.ds(idx_ref[0, 0, r], 1)], ssem.at[s])

    def wait_gather(s):
        pltpu.make_async_copy(hn_hbm.at[pl.ds(0, rows)], xbufs[s], gsem.at[s]).wait()

    def wait_scatter(s):
        pltpu.make_async_copy(ybufs[s], y_hbm.at[pl.ds(0, rows)], ssem.at[s]).wait()

    @pl.when(i == 0)
    def _():
        y1[...] = jnp.zeros(y1.shape, y1.dtype)

        def issue(r, c):
            gather_row(tok_ref, 0, r).start()
            return c
        lax.fori_loop(0, rows, issue, 0)

    def step(slot):
        other = 1 - slot

        @pl.when(i >= 1)
        def _():
            wait_scatter(slot)

        wait_gather(slot)
        for r in range(rows):
            gather_row(tokn_ref, other, r).start()
        for r in range(rows):
            scatter_row(dstp_ref, other, r).start()

        x = xbufs[slot][...].astype(BF16)
        gu = jnp.dot(x, wgu_ref[0], preferred_element_type=F32) + bgu_ref[0]
        glu = jnp.minimum(gu[:, :D_EXPERT], SWIGLU_LIMIT)
        lin = jnp.clip(gu[:, D_EXPERT:], -SWIGLU_LIMIT, SWIGLU_LIMIT)
        act = glu * (1.0 / (1.0 + jnp.exp(-SWIGLU_ALPHA * glu))) * (lin + 1.0)
        ybufs[slot][...] = jnp.dot(act.astype(BF16), wd_ref[0], preferred_element_type=F32) + bd_ref[0]

        @pl.when(i == nb - 1)
        def _():
            def issue(r, c):
                scatter_row(dst_ref, slot, r).start()
                return c
            lax.fori_loop(0, rows, issue, 0)
            wait_scatter(slot)
            wait_scatter(other)
            wait_gather(other)

    pl.when(i % 2 == 0)(functools.partial(step, 0))
    pl.when(i % 2 == 1)(functools.partial(step, 1))


def _experts(hn, block_expert, slot_tok, slot_dst, w_gu, b_gu, w_d, b_d):
    n, d = hn.shape
    n_slots = slot_tok.shape[0]
    rows = EXPERT_ROWS
    nb = n_slots // rows
    tok3 = slot_tok.reshape(nb, 1, rows)
    dst3 = slot_dst.reshape(nb, 1, rows)
    smem = lambda f: pl.BlockSpec((1, 1, rows), f, memory_space=pltpu.SMEM)
    grid_spec = pltpu.PrefetchScalarGridSpec(
        num_scalar_prefetch=1,
        grid=(nb,),
        in_specs=[smem(lambda i, be: (i, 0, 0)),
                  smem(lambda i, be: (jnp.minimum(i + 1, nb - 1), 0, 0)),
                  smem(lambda i, be: (jnp.maximum(i - 1, 0), 0, 0)),
                  smem(lambda i, be: (i, 0, 0)),
                  pl.BlockSpec(memory_space=pl.ANY),
                  pl.BlockSpec((1, d, 2 * D_EXPERT), lambda i, be: (be[i], 0, 0)),
                  pl.BlockSpec((1, 1, 2 * D_EXPERT), lambda i, be: (be[i], 0, 0)),
                  pl.BlockSpec((1, D_EXPERT, d), lambda i, be: (be[i], 0, 0)),
                  pl.BlockSpec((1, 1, d), lambda i, be: (be[i], 0, 0))],
        out_specs=pl.BlockSpec(memory_space=pl.ANY),
        scratch_shapes=[pltpu.VMEM((rows, d), F32)] * 4 +
                       [pltpu.SemaphoreType.DMA((2,)), pltpu.SemaphoreType.DMA((2,))],
    )
    return pl.pallas_call(
        _experts_kernel,
        grid_spec=grid_spec,
        out_shape=jax.ShapeDtypeStruct((n_slots, d), F32),
        compiler_params=_cparams(("arbitrary",)),
        name="experts",
    )(block_expert, tok3, tok3, dst3, dst3, hn, w_gu, b_gu, w_d, b_d)


def _routing_tables(top_idx, n_tok):
    rows = EXPERT_ROWS
    n_asg = n_tok * TOP_K
    flat_e = top_idx.reshape(-1).astype(I32)
    keys = jnp.sort(flat_e * n_asg + jnp.arange(n_asg, dtype=I32))
    order = keys % n_asg
    counts = jnp.sum(flat_e[:, None] == jnp.arange(N_EXPERTS, dtype=I32)[None, :], axis=0).astype(I32)
    padded = (counts + rows - 1) // rows * rows
    start = jnp.cumsum(counts) - counts
    pend = jnp.cumsum(padded)
    pstart = pend - padded
    nb = -(-n_asg // rows) + N_EXPERTS
    n_slots = nb * rows
    block_expert = jnp.minimum(
        jnp.searchsorted(pend, jnp.arange(nb, dtype=I32) * rows, side='right'), N_EXPERTS - 1).astype(I32)
    slot = jnp.arange(n_slots, dtype=I32)
    e = jnp.repeat(block_expert, rows)
    r = slot - pstart[e]
    valid = (r >= 0) & (r < counts[e])
    asg = order[jnp.clip(start[e] + r, 0, n_asg - 1)]
    slot_tok = jnp.where(valid, asg // TOP_K, 0).astype(I32)
    pad_rank = jnp.cumsum(jnp.logical_not(valid).astype(I32)) - 1
    slot_dst = jnp.where(valid, (asg % TOP_K) * n_tok + asg // TOP_K, n_asg + pad_rank).astype(I32)
    return block_expert, slot_tok, slot_dst


def _combine_kernel(h_ref, y0_ref, y1_ref, y2_ref, y3_ref, tg_ref, g_ref, o_ref, *, final_norm):
    h = h_ref[0]
    for kk, y_ref in enumerate((y0_ref, y1_ref, y2_ref, y3_ref)):
        h = h + y_ref[...] * tg_ref[0, :, kk:kk + 1]
    if final_norm:
        ms = jnp.mean(h * h, axis=-1, keepdims=True)
        h = h * lax.rsqrt(ms + NORM_EPS) * g_ref[...]
    o_ref[0] = h


def _combine(h1, y, tg, g, final_norm):
    batch, t, d = h1.shape
    tt = t // 6 if t % 48 == 0 else t
    per_batch = t // tt
    per_pick = batch * per_batch
    blk = lambda n: pl.BlockSpec((1, tt, n), lambda b, j: (b, j, 0))
    pick = lambda kk: pl.BlockSpec((tt, d), lambda b, j: (kk * per_pick + b * per_batch + j, 0))
    return pl.pallas_call(
        functools.partial(_combine_kernel, final_norm=final_norm),
        grid=(batch, per_batch),
        in_specs=[blk(d)] + [pick(kk) for kk in range(TOP_K)] + [blk(LANES), pl.BlockSpec((1, d), lambda b, j: (0, 0))],
        out_specs=blk(d),
        out_shape=jax.ShapeDtypeStruct((batch, t, d), F32),
        compiler_params=_cparams(("parallel", "parallel")),
        name="combine",
    )(h1, y, y, y, y, tg, g)


def _pair_blockdiag(w):
    h, a, b = w.shape
    w = w.reshape(h // 2, 2, a, b)
    z = jnp.zeros_like(w[:, 0])
    top = jnp.concatenate([w[:, 0], z], axis=2)
    bot = jnp.concatenate([z, w[:, 1]], axis=2)
    return jnp.concatenate([top, bot], axis=1)


def _uk_per_head(w_uk):
    h, a, b = w_uk.shape
    z = jnp.zeros_like(w_uk)
    even = jnp.concatenate([w_uk, z], axis=1)
    odd = jnp.concatenate([z, w_uk], axis=1)
    sel = (jnp.arange(h) % 2 == 0)[:, None, None]
    return jnp.where(sel, even, odd)


def kernel(x, meta_tokens, norm_mix_g, w_in, b_forget, q_norm_g, w_uq, kv_norm_g, w_uk, w_uv, w_qidx,
           kidx_ln_g, kidx_ln_b, w_branch_a, w_branch_b, w_out, norm_ffn_g, w_router, b_router, w_gate_up,
           b_gate_up, w_down, b_down, norm_final_g):
    batch, seq, d = x.shape
    t = seq + N_META
    tp = -(-t // Q_BLOCK) * Q_BLOCK
    k_sel = min(IDX_TOPK_MAX, seq // 4)
    depth = w_in.shape[0]
    h = jnp.concatenate([jnp.broadcast_to(meta_tokens.astype(x.dtype)[None], (batch, N_META, d)), x], axis=1)
    for l in range(depth):
        hp = jnp.pad(h, ((0, 0), (0, tp - t), (0, 0)))
        rows = batch * tp
        q, k, v, cq, small3, sga, sgb = _in_proj(
            hp.reshape(rows, d), norm_mix_g[l].astype(F32)[None], _layout_w_in(w_in[l]))
        fc = _forget_cumsum(small3, b_forget[l], batch, tp)
        fr = jnp.transpose(fc[:, :, FA_LO:FA_LO + FOX_HEADS], (0, 2, 1)).reshape(batch, FOX_HEADS // 2, 2, tp)
        sh = lambda a: a.reshape(batch, tp, a.shape[-1])
        o_a = _fox_attention(sh(q), sh(k), sh(v), fc, fr)
        qq, qidx, kk = _dsa_prep(cq, small3, tp, q_norm_g[l], kv_norm_g[l], kidx_ln_g[l], kidx_ln_b[l],
                                 _layout_w_q(w_uq[l], w_qidx[l]), _uk_per_head(w_uk[l]).astype(BF16))
        o_b = _dsa_attention(sh(qq), sh(qidx), sh(small3), sh(kk), _pair_blockdiag(w_uv[l]).astype(BF16), t, k_sel)
        w_r = jnp.pad(w_router[l], ((0, 0), (0, LANES - N_EXPERTS))).astype(BF16)
        b_r = jnp.pad(b_router[l].astype(F32), (0, LANES - N_EXPERTS), constant_values=NEG_INF)[None]
        h1, hn, ti, tg = _merge_router(
            o_a, o_b, sh(sga), sh(sgb), hp, t, w_branch_a[l].astype(BF16), w_branch_b[l].astype(BF16),
            w_out[l].astype(BF16), norm_ffn_g[l].astype(F32)[None], w_r, b_r)
        n_tok = batch * t
        block_expert, slot_tok, slot_dst = _routing_tables(ti[:, :, :TOP_K], n_tok)
        y = _experts(hn.reshape(n_tok, d), block_expert, slot_tok, slot_dst,
                     w_gate_up[l].astype(BF16), b_gate_up[l].astype(F32)[:, None, :],
                     w_down[l].astype(BF16), b_down[l].astype(F32)[:, None, :])
        h = _combine(h1, y, tg, norm_final_g.astype(F32)[None], final_norm=(l == depth - 1))
    return h[:, N_META:]
```

```python
import functools

import numpy as np
import jax
import jax.numpy as jnp
from jax import lax
from jax.experimental import pallas as pl
from jax.experimental.pallas import tpu as pltpu

F32 = jnp.float32
BF16 = jnp.bfloat16
I32 = jnp.int32

D_MODEL = 1024
CHUNK = 64
N_META = 16
Q_BLOCK = 128
ROPE_THETA = 10000.0
NORM_EPS = 1e-6
NEG_INF = -1e30

FOX_HEADS = 8
FOX_HEAD_DIM = 64
DSA_HEADS = 8
Q_RANK = 256
KV_RANK = 128
QK_NOPE_DIM = 64
QK_ROPE_DIM = 32
V_HEAD_DIM = 64
IDX_HEADS = 4
IDX_HEAD_DIM = 64
IDX_ROPE_DIM = 32
IDX_TOPK_MAX = 256
N_EXPERTS = 32
TOP_K = 4
D_EXPERT = D_MODEL
SWIGLU_ALPHA = 1.702
SWIGLU_LIMIT = 7.0

FOX_WIDTH = FOX_HEADS * FOX_HEAD_DIM
DSA_WIDTH = DSA_HEADS * V_HEAD_DIM
IN_SPLITS = (FOX_WIDTH, FOX_WIDTH, FOX_WIDTH, FOX_HEADS, Q_RANK, KV_RANK, QK_ROPE_DIM,
             IDX_HEAD_DIM, IDX_HEADS, D_MODEL, D_MODEL)

LANES = 128
VMEM_LIMIT = 56 * 1024 * 1024
KEY_GROUP = 512
KEY_TILE = 256
Q_ROWS = 256
LOG2E = 1.4426950408889634
EXPERT_ROWS = 256

KR_LO, KIDX_LO = 0, 32
FA_LO, IW_LO = 64, 72

C_Q, C_K, C_V = 0, 512, 1024
C_CQ = 1536
C_G1, C_G2, C_G3 = 1792, 1920, 2048
C_GA, C_GB = 2176, 3200
C_TOTAL = 4224


def _cparams(sem):
    return pltpu.CompilerParams(dimension_semantics=sem, vmem_limit_bytes=VMEM_LIMIT)


def _row_tile(rows, cap):
    t = cap
    while rows % t:
        t //= 2
    return t


def _rot_cols(w):
    h = w.shape[-1] // 2
    return jnp.concatenate([-w[..., h:], w[..., :h]], axis=-1)


def _perm_cols(w):
    h = w.shape[-1] // 2
    return jnp.concatenate([w[..., h:], w[..., :h]], axis=-1)


def _layout_w_in(w_in):
    offs = np.cumsum((0,) + IN_SPLITS)
    qa, ka, va, fa, cq, ckv, kr, kidx, iw, ga, gb = [w_in[:, offs[i]:offs[i + 1]] for i in range(11)]
    d = w_in.shape[0]
    z = lambda n: jnp.zeros((d, n), w_in.dtype)
    g2 = jnp.concatenate([kr, kidx, z(32)], axis=1)
    g3 = jnp.concatenate([_rot_cols(kr), _perm_cols(kidx[:, :IDX_ROPE_DIM]), fa, iw, z(52)], axis=1)
    w = jnp.concatenate([qa, ka, va, cq, ckv, g2, g3, ga, gb], axis=1)
    assert w.shape[1] == C_TOTAL
    return w.astype(BF16)


def _layout_w_q(w_uq, w_qidx):
    r = w_uq.shape[0]
    wq = w_uq.reshape(r, DSA_HEADS, QK_NOPE_DIM + QK_ROPE_DIM)
    nope = wq[:, :, :QK_NOPE_DIM].reshape(r, DSA_HEADS * QK_NOPE_DIM)
    rope = wq[:, :, QK_NOPE_DIM:]
    pad_r = lambda a: jnp.pad(a, ((0, 0), (0, 0), (KR_LO, LANES - KR_LO - QK_ROPE_DIM))).reshape(r, -1)
    wi = w_qidx.reshape(r, IDX_HEADS, IDX_HEAD_DIM)
    wi_rot = jnp.concatenate([_rot_cols(wi[:, :, :IDX_ROPE_DIM]), jnp.zeros_like(wi[:, :, IDX_ROPE_DIM:])], axis=-1)
    pad_i = lambda a: jnp.pad(a, ((0, 0), (0, 0), (KIDX_LO, LANES - KIDX_LO - IDX_HEAD_DIM))).reshape(r, -1)
    w = jnp.concatenate([nope, pad_r(rope), pad_r(_rot_cols(rope)), pad_i(wi), pad_i(wi_rot)], axis=1)
    return w.astype(BF16)


def _rope_tables(tp):
    half = QK_ROPE_DIM // 2
    inv = ROPE_THETA ** (-jnp.arange(half, dtype=F32) / half)
    ang = jnp.arange(tp, dtype=F32)[:, None] * inv[None, :]
    cos, sin = jnp.cos(ang), jnp.sin(ang)
    cos32 = jnp.concatenate([cos, cos], axis=1)
    sin32 = jnp.concatenate([sin, sin], axis=1)
    one, zero = jnp.ones((tp, 1), F32), jnp.zeros((tp, 1), F32)
    cos_k = jnp.concatenate([cos32, cos32, jnp.tile(one, (1, 64))], axis=1)
    sin_k = jnp.concatenate([sin32, sin32, jnp.tile(zero, (1, 64))], axis=1)
    return cos_k, sin_k


def _inproj_kernel(x_ref, g_ref, w_ref, q_ref, k_ref, v_ref, cq_ref, sm_ref, ga_ref, gb_ref):
    x = x_ref[...]
    ms = jnp.mean(x * x, axis=-1, keepdims=True)
    xn = (x * lax.rsqrt(ms + NORM_EPS) * g_ref[...]).astype(BF16)

    def mm(lo, hi):
        return jnp.dot(xn, w_ref[:, lo:hi], preferred_element_type=F32)

    q_ref[...] = (mm(C_Q, C_K) * (FOX_HEAD_DIM ** -0.5 * LOG2E)).astype(BF16)
    k_ref[...] = mm(C_K, C_V).astype(BF16)
    lane = lax.broadcasted_iota(I32, (x.shape[0], LANES), 1)
    ones_lane = jnp.where(lane == 0, 1.0, 0.0).astype(BF16)
    for p in range(FOX_HEADS // 2):
        v_ref[:, 2 * p * LANES:(2 * p + 1) * LANES] = mm(C_V + p * LANES, C_V + (p + 1) * LANES).astype(BF16)
        v_ref[:, (2 * p + 1) * LANES:(2 * p + 2) * LANES] = ones_lane
    cq_ref[...] = mm(C_CQ, C_G1)
    sm_ref[...] = mm(C_G1, C_GA)
    ga_ref[...] = (1.0 / (1.0 + jnp.exp(-mm(C_GA, C_GB)))).astype(BF16)
    gb_ref[...] = (1.0 / (1.0 + jnp.exp(-mm(C_GB, C_TOTAL)))).astype(BF16)


def _in_proj(hp, g, w_cat):
    rows, d = hp.shape
    tm = _row_tile(rows, 512)
    row = lambda n: pl.BlockSpec((tm, n), lambda i: (i, 0))
    full = lambda a: pl.BlockSpec(a.shape, lambda i: (0,) * a.ndim)
    outs = [(FOX_WIDTH, BF16), (FOX_WIDTH, BF16), (2 * FOX_WIDTH, BF16), (Q_RANK, F32), (3 * LANES, F32),
            (D_MODEL, BF16), (D_MODEL, BF16)]
    return pl.pallas_call(
        _inproj_kernel,
        grid=(rows // tm,),
        in_specs=[row(d), full(g), full(w_cat)],
        out_specs=[row(n) for n, _ in outs],
        out_shape=[jax.ShapeDtypeStruct((rows, n), dt) for n, dt in outs],
        compiler_params=_cparams(("parallel",)),
        name="in_proj",
    )(hp, g, w_cat)


def _split3(x):
    hi = x.astype(BF16)
    r1 = x - hi.astype(F32)
    mid = r1.astype(BF16)
    lo = (r1 - mid.astype(F32)).astype(BF16)
    return hi, mid, lo


def _fcum_kernel(sm_ref, b_ref, tri_ref, o_ref):
    tp = o_ref.shape[1]
    tri = tri_ref[...]
    carry = jnp.zeros((1, LANES), F32)
    for j in range(tp // LANES):
        z = sm_ref[0, j * LANES:(j + 1) * LANES, :] + b_ref[...]
        ls = jnp.minimum(z, 0.0) - jnp.log(1.0 + jnp.exp(-jnp.abs(z)))
        c = carry
        for part in _split3(ls):
            c = c + jnp.dot(tri, part, preferred_element_type=F32)
        o_ref[0, j * LANES:(j + 1) * LANES, :] = c * LOG2E
        carry = c[LANES - 1:LANES, :]


def _forget_cumsum(small3, b_forget, batch, tp):
    g3 = small3.reshape(batch, tp, 3 * LANES)
    bias = jnp.zeros((1, LANES), F32).at[0, FA_LO:FA_LO + FOX_HEADS].set(b_forget.astype(F32))
    tri = jnp.tril(jnp.ones((LANES, LANES), F32)).astype(BF16)
    return pl.pallas_call(
        _fcum_kernel,
        grid=(batch,),
        in_specs=[pl.BlockSpec((1, tp, LANES), lambda b: (b, 0, 2)),
                  pl.BlockSpec((1, LANES), lambda b: (0, 0)),
                  pl.BlockSpec((LANES, LANES), lambda b: (0, 0))],
        out_specs=pl.BlockSpec((1, tp, LANES), lambda b: (b, 0, 0)),
        out_shape=jax.ShapeDtypeStruct((batch, tp, LANES), F32),
        compiler_params=_cparams(("parallel",)),
        name="fcum",
    )(g3, bias, tri)


def _key_widths(tp):
    ws = list(range(KEY_GROUP, tp, KEY_GROUP)) + [tp]
    return ws


def _query_steps(tp):
    return -(-tp // Q_ROWS)


def _for_key_width(qi, tp, body, lookahead=0):
    groups = []
    for step in range(_query_steps(tp)):
        rows = min(Q_ROWS, tp - step * Q_ROWS)
        need = min(tp, step * Q_ROWS + rows + lookahead)
        w = min(x for x in _key_widths(tp) if x >= need)
        if groups and groups[-1][:2] == [rows, w]:
            groups[-1][3] = step
        else:
            groups.append([rows, w, step, step])
    for rows, w, first, last in groups:
        pl.when((qi >= first) & (qi <= last))(functools.partial(body, rows, w, first))


def _key_tiles(w):
    return [(k0, min(KEY_TILE, w - k0)) for k0 in range(0, w, KEY_TILE)]


def _lane_group_max(mx, s):
    for c in range(s.shape[1] // LANES):
        mx = jnp.maximum(mx, s[:, c * LANES:(c + 1) * LANES])
    return mx


def _fox_kernel(q_ref, kt_ref, v_ref, fr_ref, o_ref, s_ref):
    qi = pl.program_id(2)
    tp = kt_ref.shape[3]

    def body(rows, w, first):
        q = q_ref[0, :rows]
        lane = lax.broadcasted_iota(I32, (rows, LANES), 1)
        zero = jnp.zeros_like(q)
        qs = jnp.concatenate([jnp.where(lane < FOX_HEAD_DIM, q, zero),
                              jnp.where(lane >= FOX_HEAD_DIM, q, zero)], axis=0)
        mxs = [jnp.full((rows, LANES), -jnp.inf, F32)] * 2
        for k0, tk in _key_tiles(w):
            s2 = jnp.dot(qs, kt_ref[0, 0, :, k0:k0 + tk], preferred_element_type=F32)
            for hh in range(2):
                s = s2[hh * rows:(hh + 1) * rows] - fr_ref[0, 0, hh:hh + 1, k0:k0 + tk]
                if k0 + tk > first * Q_ROWS:
                    qpos = qi * Q_ROWS + lax.broadcasted_iota(I32, (rows, tk), 0)
                    kpos = k0 + lax.broadcasted_iota(I32, (rows, tk), 1)
                    s = jnp.where(kpos <= qpos, s, NEG_INF)
                s_ref[hh, :rows, k0:k0 + tk] = s
                mxs[hh] = _lane_group_max(mxs[hh], s)
        ms = [jnp.max(mx, axis=-1, keepdims=True) for mx in mxs]
        acc = jnp.zeros((2 * rows, 2 * LANES), F32)
        for k0, tk in _key_tiles(w):
            p2 = jnp.concatenate([jnp.exp2((s_ref[hh, :rows, k0:k0 + tk] - ms[hh]).astype(BF16))
                                  for hh in range(2)], axis=0)
            acc = acc + jnp.dot(p2, v_ref[0, k0:k0 + tk, :], preferred_element_type=F32)
        o = acc[:, :LANES] * (1.0 / acc[:, LANES:LANES + 1])
        o_ref[0, :rows] = jnp.where(lane < FOX_HEAD_DIM, o[:rows], o[rows:]).astype(BF16)

    _for_key_width(qi, tp, body)


def _fox_attention(q, kt, v, fr):
    batch, tp, _ = q.shape
    nq = _query_steps(tp)
    pairs = FOX_HEADS // 2
    return pl.pallas_call(
        _fox_kernel,
        grid=(batch, pairs, nq),
        in_specs=[pl.BlockSpec((1, Q_ROWS, LANES), lambda b, p, i: (b, i, p)),
                  pl.BlockSpec((1, 1, LANES, tp), lambda b, p, i: (b, p, 0, 0)),
                  pl.BlockSpec((1, tp, 2 * LANES), lambda b, p, i: (b, 0, p)),
                  pl.BlockSpec((1, 1, 2, tp), lambda b, p, i: (b, p, 0, 0))],
        out_specs=pl.BlockSpec((1, Q_ROWS, LANES), lambda b, p, i: (b, i, p)),
        out_shape=jax.ShapeDtypeStruct((batch, tp, FOX_WIDTH), BF16),
        scratch_shapes=[pltpu.VMEM((2, Q_ROWS, tp), F32)],
        compiler_params=_cparams(("parallel", "parallel", "arbitrary")),
        name="fox",
    )(q, kt, v, fr)


def _dsa_prep_kernel(cq_ref, sm_ref, qg_ref, kvg_ref, lng_ref, lnb_ref, lngp_ref, lnbp_ref,
                     cosk_ref, sink_ref, wq_ref, wuk_ref, qq_ref, qidx_ref, kk_ref):
    cq = cq_ref[...]
    ms = jnp.mean(cq * cq, axis=-1, keepdims=True)
    cqn = (cq * lax.rsqrt(ms + NORM_EPS) * qg_ref[...]).astype(BF16)
    cos = cosk_ref[...]
    sin = sink_ref[...]

    def mm(lo, hi):
        return jnp.dot(cqn, wq_ref[:, lo:hi], preferred_element_type=F32)

    nope_w = DSA_HEADS * QK_NOPE_DIM
    hw = DSA_HEADS * LANES
    iw = IDX_HEADS * LANES
    q_nope = mm(0, nope_w).astype(BF16)
    for h in range(DSA_HEADS):
        base = nope_w + h * LANES
        q_rope = mm(base, base + LANES) * cos + mm(base + hw, base + hw + LANES) * sin
        p = h // 2
        q_lat = jnp.dot(q_nope[:, p * LANES:(p + 1) * LANES], wuk_ref[h], preferred_element_type=F32)
        qscale = (QK_NOPE_DIM + QK_ROPE_DIM) ** -0.5 * LOG2E
        qq_ref[:, h * 2 * LANES:h * 2 * LANES + LANES] = (q_lat * qscale).astype(BF16)
        qq_ref[:, h * 2 * LANES + LANES:(h + 1) * 2 * LANES] = (q_rope * qscale).astype(BF16)
    for h in range(IDX_HEADS):
        base = nope_w + 2 * hw + h * LANES
        qi = mm(base, base + LANES) * cos + mm(base + iw, base + iw + LANES) * sin
        qidx_ref[:, h * LANES:(h + 1) * LANES] = qi.astype(BF16)

    ckv = sm_ref[:, 0:LANES]
    ms = jnp.mean(ckv * ckv, axis=-1, keepdims=True)
    kk_ref[:, 0:LANES] = (ckv * lax.rsqrt(ms + NORM_EPS) * kvg_ref[...]).astype(BF16)
    g2 = sm_ref[:, LANES:2 * LANES]
    g3 = sm_ref[:, 2 * LANES:3 * LANES]
    lane = lax.broadcasted_iota(I32, g2.shape, 1)
    is_idx = (lane >= KIDX_LO) & (lane < KIDX_LO + IDX_HEAD_DIM)
    is_rope = lane < KIDX_LO
    xi = jnp.where(is_idx, g2, 0.0)
    mu = jnp.sum(xi, axis=-1, keepdims=True) * (1.0 / IDX_HEAD_DIM)
    xc = jnp.where(is_idx, g2 - mu, 0.0)
    var = jnp.sum(xc * xc, axis=-1, keepdims=True) * (1.0 / IDX_HEAD_DIM)
    rstd = lax.rsqrt(var + NORM_EPS)
    y = xc * rstd * lng_ref[...] + lnb_ref[...]
    yp = (g3 - mu) * rstd * lngp_ref[...] + lnbp_ref[...]
    a = jnp.where(is_rope, g2, y)
    b = jnp.where(is_rope, g3, jnp.where(lane < KIDX_LO + IDX_ROPE_DIM, yp, 0.0))
    kx = jnp.where(lane == LANES - 1, 1.0, a * cos + b * sin)
    kk_ref[:, LANES:2 * LANES] = kx.astype(BF16)


def _dsa_prep(cq, small3, tp, q_norm_g, kv_norm_g, ln_g, ln_b, w_q, w_uk_pair):
    rows = cq.shape[0]
    tm = tp // 4 if tp % 64 == 0 else tp
    per = tp // tm
    cos_k, sin_k = _rope_tables(tp)
    lane_vec = lambda v, lo: jnp.zeros((1, LANES), F32).at[0, lo:lo + v.shape[0]].set(v.astype(F32))
    half = IDX_ROPE_DIM // 2
    sign = jnp.concatenate([-jnp.ones((half,), F32), jnp.ones((half,), F32)])
    g_perm = jnp.concatenate([ln_g[half:IDX_ROPE_DIM], ln_g[:half]]).astype(F32) * sign
    b_perm = jnp.concatenate([ln_b[half:IDX_ROPE_DIM], ln_b[:half]]).astype(F32) * sign
    row = lambda n: pl.BlockSpec((tm, n), lambda i: (i, 0))
    tab = pl.BlockSpec((tm, LANES), lambda i: (i % per, 0))
    full = lambda a: pl.BlockSpec(a.shape, lambda i: (0,) * a.ndim)
    vecs = [q_norm_g.astype(F32)[None], kv_norm_g.astype(F32)[None], lane_vec(ln_g, KIDX_LO),
            lane_vec(ln_b, KIDX_LO), lane_vec(g_perm, KIDX_LO), lane_vec(b_perm, KIDX_LO)]
    outs = [(DSA_HEADS * 2 * LANES, BF16), (IDX_HEADS * LANES, BF16), (2 * LANES, BF16)]
    return pl.pallas_call(
        _dsa_prep_kernel,
        grid=(rows // tm,),
        in_specs=[row(Q_RANK), row(3 * LANES)] + [full(v) for v in vecs] + [tab, tab, full(w_q), full(w_uk_pair)],
        out_specs=[row(n) for n, _ in outs],
        out_shape=[jax.ShapeDtypeStruct((rows, n), dt) for n, dt in outs],
        compiler_params=_cparams(("parallel",)),
        name="dsa_prep",
    )(cq, small3, *vecs, cos_k, sin_k, w_q, w_uk_pair)


def _dsa_kernel(qq_ref, qidx_ref, sm_ref, kk_ref, kkt_ref, tri_ref, wuv_ref, o_ref, key_ref, bias_ref, s_ref, *,
                n_valid, k_sel):
    qi = pl.program_id(1)
    tp = kk_ref.shape[1]
    int_min = jnp.int32(-2 ** 31)
    chain = Q_BLOCK

    def body(rows, w, first):
        del first
        qpos = qi * Q_ROWS + lax.broadcasted_iota(I32, (rows, 1), 0)
        kend = (((qpos + (CHUNK - N_META)) // CHUNK) + 1) * CHUNK - (CHUNK - N_META)
        kend = jnp.minimum(kend, n_valid)
        whs = [sm_ref[0, :rows, IW_LO + h:IW_LO + h + 1] * (IDX_HEADS ** -0.5) for h in range(IDX_HEADS)]
        qidx_all = jnp.concatenate([qidx_ref[0, :rows, h * LANES:(h + 1) * LANES] for h in range(IDX_HEADS)], axis=0)

        for k0, tk in _key_tiles(w):
            d_all = jnp.dot(qidx_all, kkt_ref[0, LANES:, k0:k0 + tk], preferred_element_type=F32)
            acc = None
            for h in range(IDX_HEADS):
                term = whs[h] * jnp.maximum(d_all[h * rows:(h + 1) * rows], 0.0)
                acc = term if acc is None else acc + term
            isc = acc * (IDX_HEAD_DIM ** -0.5)
            isc = jnp.where(isc == 0.0, 0.0, isc)
            adm = (k0 + lax.broadcasted_iota(I32, (rows, tk), 1)) < kend
            bits = pltpu.bitcast(isc, I32)
            key = bits ^ ((bits >> 31) & jnp.int32(0x7FFFFFFF))
            key_ref[:rows, k0:k0 + tk] = jnp.where(adm, key, int_min)
            bias_ref[:rows, k0:k0 + tk] = jnp.where(adm & (isc > 0.5 * NEG_INF), 0.0, NEG_INF)

        starts = tuple(range(0, rows, chain))

        def count(pred, r0):
            return jnp.sum(jnp.where(pred(key_ref[r0:r0 + chain, :w]), 1.0, 0.0), axis=-1, keepdims=True)

        kf = jnp.float32(k_sel)
        zero = jnp.zeros((chain, 1), I32)
        t0 = tuple(jnp.where(count(lambda k: k >= zero, r0) >= kf, jnp.int32(0), int_min) for r0 in starts)

        def bit_step(i, ts):
            bit = jnp.left_shift(jnp.int32(1), 30 - i)
            out = []
            for t, r0 in zip(ts, starts):
                cand = t | bit
                out.append(jnp.where(count(lambda k, c=cand: k >= c, r0) >= kf, cand, t))
            return tuple(out)

        thrs = lax.fori_loop(0, 31, bit_step, t0)
        thr = jnp.concatenate(thrs, axis=0) if len(thrs) > 1 else thrs[0]
        n_gts = [count(lambda k, t=t: k > t, r0) for t, r0 in zip(thrs, starts)]
        need = kf - (jnp.concatenate(n_gts, axis=0) if len(n_gts) > 1 else n_gts[0])

        run = jnp.zeros((rows, 1), F32)
        tri = tri_ref[...]
        for j in range(w // LANES):
            sl = slice(j * LANES, (j + 1) * LANES)
            key = key_ref[:rows, sl]
            eq = key == thr
            pre = jnp.dot(jnp.where(eq, 1.0, 0.0).astype(BF16), tri, preferred_element_type=F32) + run
            take = (key > thr) | (eq & (pre <= need))
            bias_ref[:rows, sl] = jnp.where(take, bias_ref[:rows, sl], NEG_INF)
            run = pre[:, LANES - 1:LANES]

        for p in range(DSA_HEADS // 2):
            heads = (2 * p, 2 * p + 1)
            q2 = jnp.concatenate([qq_ref[0, :rows, h * 2 * LANES:(h + 1) * 2 * LANES] for h in heads], axis=0)
            mxs = [jnp.full((rows, LANES), -jnp.inf, F32)] * 2
            for k0, tk in _key_tiles(w):
                s2 = jnp.dot(q2, kkt_ref[0, :, k0:k0 + tk], preferred_element_type=F32)
                for hh in range(2):
                    s = s2[hh * rows:(hh + 1) * rows] + bias_ref[:rows, k0:k0 + tk]
                    s_ref[2 * (p % 2) + hh, :rows, k0:k0 + tk] = s
                    mxs[hh] = _lane_group_max(mxs[hh], s)
            ms = [jnp.max(mx, axis=-1, keepdims=True) for mx in mxs]
            acc = jnp.zeros((2 * rows, 2 * LANES), F32)
            for k0, tk in _key_tiles(w):
                p2 = jnp.concatenate(
                    [jnp.exp2((s_ref[2 * (p % 2) + hh, :rows, k0:k0 + tk] - ms[hh]).astype(BF16))
                     for hh in range(2)], axis=0)
                acc = acc + jnp.dot(p2, kk_ref[0, k0:k0 + tk, :], preferred_element_type=F32)
            lat = (acc[:, :LANES] * (1.0 / acc[:, 2 * LANES - 1:2 * LANES])).astype(BF16)
            o_lat = jnp.concatenate([lat[:rows], lat[rows:]], axis=1)
            o_ref[0, :rows, p * LANES:(p + 1) * LANES] = jnp.dot(
                o_lat, wuv_ref[p], preferred_element_type=F32).astype(BF16)

    _for_key_width(qi, tp, body, lookahead=Q_BLOCK)


def _dsa_attention(qq, qidx, small3, kk, w_uv_pair, n_valid, k_sel):
    batch, tp, _ = kk.shape
    nq = _query_steps(tp)
    tri = jnp.triu(jnp.ones((LANES, LANES), F32)).astype(BF16)
    qblk = lambda n: pl.BlockSpec((1, Q_ROWS, n), lambda b, i: (b, i, 0))
    return pl.pallas_call(
        functools.partial(_dsa_kernel, n_valid=n_valid, k_sel=k_sel),
        grid=(batch, nq),
        in_specs=[qblk(DSA_HEADS * 2 * LANES), qblk(IDX_HEADS * LANES),
                  pl.BlockSpec((1, Q_ROWS, LANES), lambda b, i: (b, i, 2)),
                  pl.BlockSpec((1, tp, 2 * LANES), lambda b, i: (b, 0, 0)),
                  pl.BlockSpec((1, 2 * LANES, tp), lambda b, i: (b, 0, 0)),
                  pl.BlockSpec((LANES, LANES), lambda b, i: (0, 0)),
                  pl.BlockSpec(w_uv_pair.shape, lambda b, i: (0, 0, 0))],
        out_specs=qblk(DSA_WIDTH),
        out_shape=jax.ShapeDtypeStruct((batch, tp, DSA_WIDTH), BF16),
        scratch_shapes=[pltpu.VMEM((Q_ROWS, tp), I32), pltpu.VMEM((Q_ROWS, tp), F32),
                        pltpu.VMEM((4, Q_ROWS, tp), F32)],
        compiler_params=_cparams(("parallel", "arbitrary")),
        name="dsa",
    )(qq, qidx, small3, kk, jnp.transpose(kk, (0, 2, 1)), tri, w_uv_pair)


def _store_row_tiles(ref, x):
    rows, d = x.shape
    sub = d // LANES
    for j in range(sub):
        ref[pl.ds(j, rows, stride=sub), :] = x[:, j * LANES:(j + 1) * LANES]


def _load_row_tiles(ref, rows):
    sub = ref.shape[0] // rows
    return jnp.concatenate([ref[pl.ds(j, rows, stride=sub), :] for j in range(sub)], axis=1)


def _merge_kernel(oa_ref, ob_ref, ga_ref, gb_ref, h_ref, wa_ref, wb_ref, wo_ref, g_ref, wr_ref, br_ref,
                  h1_ref, hn_ref, ti_ref, tg_ref):
    a = jnp.dot(oa_ref[0], wa_ref[...], preferred_element_type=F32)
    b = jnp.dot(ob_ref[0], wb_ref[...], preferred_element_type=F32)
    merged = ga_ref[0].astype(F32) * a + gb_ref[0].astype(F32) * b
    mix = jnp.dot(merged.astype(BF16), wo_ref[...], preferred_element_type=F32)
    h1 = h_ref[0] + mix
    h1_ref[0] = h1
    ms = jnp.mean(h1 * h1, axis=-1, keepdims=True)
    hn = h1 * lax.rsqrt(ms + NORM_EPS) * g_ref[...]
    _store_row_tiles(hn_ref.at[0], hn)
    logits = jnp.dot(hn.astype(BF16), wr_ref[...], preferred_element_type=F32) + br_ref[...]
    lane = lax.broadcasted_iota(I32, logits.shape, 1)
    idx_out = jnp.zeros(logits.shape, I32)
    val_out = jnp.zeros(logits.shape, F32)
    vals = []
    for kk in range(TOP_K):
        m = jnp.max(logits, axis=-1, keepdims=True)
        idx = jnp.min(jnp.where(logits == m, lane, LANES), axis=-1, keepdims=True)
        logits = jnp.where(lane == idx, -jnp.inf, logits)
        idx_out = jnp.where(lane == kk, idx, idx_out)
        vals.append(m)
    es = [jnp.exp(v - vals[0]) for v in vals]
    den = es[0] + es[1] + es[2] + es[3]
    for kk in range(TOP_K):
        val_out = jnp.where(lane == kk, es[kk] / den, val_out)
    ti_ref[0] = idx_out
    tg_ref[0] = val_out


def _merge_router(o_a, o_b, sga, sgb, hp, t_valid, w_a, w_b, w_o, g_ffn, w_r, b_r):
    batch, tp, d = hp.shape
    tt = t_valid // 3 if t_valid % 48 == 0 else t_valid
    blk = lambda n: pl.BlockSpec((1, tt, n), lambda b, j: (b, j, 0))
    full = lambda a: pl.BlockSpec(a.shape, lambda b, j: (0,) * a.ndim)
    consts = [w_a, w_b, w_o, g_ffn, w_r, b_r]
    sub = d // LANES
    return pl.pallas_call(
        _merge_kernel,
        grid=(batch, t_valid // tt),
        in_specs=[blk(FOX_WIDTH), blk(DSA_WIDTH), blk(d), blk(d), blk(d)] + [full(c) for c in consts],
        out_specs=[blk(d), pl.BlockSpec((1, tt * sub, LANES), lambda b, j: (b, j, 0)), blk(LANES), blk(LANES)],
        out_shape=[jax.ShapeDtypeStruct((batch, t_valid, d), F32),
                   jax.ShapeDtypeStruct((batch, t_valid * sub, LANES), F32),
                   jax.ShapeDtypeStruct((batch, t_valid, LANES), I32),
                   jax.ShapeDtypeStruct((batch, t_valid, LANES), F32)],
        compiler_params=_cparams(("parallel", "parallel")),
        name="merge_router",
    )(o_a, o_b, sga, sgb, hp, *consts)


def _experts_kernel(be_ref, tok_ref, tokn_ref, dstp_ref, dst_ref, hn_hbm, wgu_ref, bgu_ref, wd_ref, bd_ref,
                    y_hbm, x0, x1, y0, y1, gsem, ssem):
    del be_ref
    i = pl.program_id(0)
    nb = pl.num_programs(0)
    sub = wd_ref.shape[2] // LANES
    rows = x0.shape[0] // sub
    xbufs, ybufs = (x0, x1), (y0, y1)

    def gather_row(idx_ref, s, r):
        src = pl.multiple_of(idx_ref[0, 0, r], sub)
        return pltpu.make_async_copy(hn_hbm.at[pl.ds(src, sub)], xbufs[s].at[pl.ds(r * sub, sub)], gsem.at[s])

    def scatter_row(idx_ref, s, r):
        dst = pl.multiple_of(idx_ref[0, 0, r], sub)
        return pltpu.make_async_copy(ybufs[s].at[pl.ds(r * sub, sub)], y_hbm.at[pl.ds(dst, sub)], ssem.at[s])

    def wait_gather(s):
        pltpu.make_async_copy(hn_hbm.at[pl.ds(0, rows * sub)], xbufs[s], gsem.at[s]).wait()

    def wait_scatter(s):
        pltpu.make_async_copy(ybufs[s], y_hbm.at[pl.ds(0, rows * sub)], ssem.at[s]).wait()

    @pl.when(i == 0)
    def _():
        y1[...] = jnp.zeros(y1.shape, y1.dtype)

        def issue(r, c):
            gather_row(tok_ref, 0, r).start()
            return c
        lax.fori_loop(0, rows, issue, 0)

    def step(slot):
        other = 1 - slot

        @pl.when(i >= 1)
        def _():
            wait_scatter(slot)

        wait_gather(slot)
        for r in range(rows):
            gather_row(tokn_ref, other, r).start()
        for r in range(rows):
            scatter_row(dstp_ref, other, r).start()

        x = _load_row_tiles(xbufs[slot], rows).astype(BF16)
        gu = jnp.dot(x, wgu_ref[0], preferred_element_type=F32) + bgu_ref[0]
        glu = jnp.minimum(gu[:, :D_EXPERT], SWIGLU_LIMIT)
        lin = jnp.clip(gu[:, D_EXPERT:], -SWIGLU_LIMIT, SWIGLU_LIMIT)
        act = glu * (1.0 / (1.0 + jnp.exp(-SWIGLU_ALPHA * glu))) * (lin + 1.0)
        _store_row_tiles(ybufs[slot], jnp.dot(act.astype(BF16), wd_ref[0], preferred_element_type=F32) + bd_ref[0])

        @pl.when(i == nb - 1)
        def _():
            def issue(r, c):
                scatter_row(dst_ref, slot, r).start()
                return c
            lax.fori_loop(0, rows, issue, 0)
            wait_scatter(slot)
            wait_scatter(other)
            wait_gather(other)

    pl.when(i % 2 == 0)(functools.partial(step, 0))
    pl.when(i % 2 == 1)(functools.partial(step, 1))


def _experts(hn, block_expert, slot_tok, slot_dst, w_gu, b_gu, w_d, b_d):
    d = w_d.shape[2]
    sub = d // LANES
    n_slots = slot_tok.shape[0]
    rows = EXPERT_ROWS
    nb = n_slots // rows
    tok3 = (slot_tok * sub).reshape(nb, 1, rows)
    dst3 = (slot_dst * sub).reshape(nb, 1, rows)
    smem = lambda f: pl.BlockSpec((1, 1, rows), f, memory_space=pltpu.SMEM)
    grid_spec = pltpu.PrefetchScalarGridSpec(
        num_scalar_prefetch=1,
        grid=(nb,),
        in_specs=[smem(lambda i, be: (i, 0, 0)),
                  smem(lambda i, be: (jnp.minimum(i + 1, nb - 1), 0, 0)),
                  smem(lambda i, be: (jnp.maximum(i - 1, 0), 0, 0)),
                  smem(lambda i, be: (i, 0, 0)),
                  pl.BlockSpec(memory_space=pl.ANY),
                  pl.BlockSpec((1, d, 2 * D_EXPERT), lambda i, be: (be[i], 0, 0)),
                  pl.BlockSpec((1, 1, 2 * D_EXPERT), lambda i, be: (be[i], 0, 0)),
                  pl.BlockSpec((1, D_EXPERT, d), lambda i, be: (be[i], 0, 0)),
                  pl.BlockSpec((1, 1, d), lambda i, be: (be[i], 0, 0))],
        out_specs=pl.BlockSpec(memory_space=pl.ANY),
        scratch_shapes=[pltpu.VMEM((rows * sub, LANES), F32)] * 4 +
                       [pltpu.SemaphoreType.DMA((2,)), pltpu.SemaphoreType.DMA((2,))],
    )
    return pl.pallas_call(
        _experts_kernel,
        grid_spec=grid_spec,
        out_shape=jax.ShapeDtypeStruct((n_slots * sub, LANES), F32),
        compiler_params=_cparams(("arbitrary",)),
        name="experts",
    )(block_expert, tok3, tok3, dst3, dst3, hn, w_gu, b_gu, w_d, b_d)


def _routing_tables(top_idx, n_tok):
    rows = EXPERT_ROWS
    n_asg = n_tok * TOP_K
    flat_e = top_idx.reshape(-1).astype(I32)
    keys = jnp.sort(flat_e * n_asg + jnp.arange(n_asg, dtype=I32))
    order = keys % n_asg
    counts = jnp.sum(flat_e[:, None] == jnp.arange(N_EXPERTS, dtype=I32)[None, :], axis=0).astype(I32)
    padded = (counts + rows - 1) // rows * rows
    start = jnp.cumsum(counts) - counts
    pend = jnp.cumsum(padded)
    pstart = pend - padded
    nb = -(-n_asg // rows) + N_EXPERTS
    n_slots = nb * rows
    block_expert = jnp.minimum(
        jnp.searchsorted(pend, jnp.arange(nb, dtype=I32) * rows, side='right'), N_EXPERTS - 1).astype(I32)
    slot = jnp.arange(n_slots, dtype=I32)
    e = jnp.repeat(block_expert, rows)
    r = slot - pstart[e]
    valid = (r >= 0) & (r < counts[e])
    asg = order[jnp.clip(start[e] + r, 0, n_asg - 1)]
    slot_tok = jnp.where(valid, asg // TOP_K, 0).astype(I32)
    pad_rank = jnp.cumsum(jnp.logical_not(valid).astype(I32)) - 1
    slot_dst = jnp.where(valid, (asg % TOP_K) * n_tok + asg // TOP_K, n_asg + pad_rank).astype(I32)
    return block_expert, slot_tok, slot_dst


def _combine_kernel(h_ref, y0_ref, y1_ref, y2_ref, y3_ref, tg_ref, g_ref, o_ref, *, final_norm):
    h = h_ref[0]
    for kk, y_ref in enumerate((y0_ref, y1_ref, y2_ref, y3_ref)):
        h = h + _load_row_tiles(y_ref, h.shape[0]) * tg_ref[0, :, kk:kk + 1]
    if final_norm:
        ms = jnp.mean(h * h, axis=-1, keepdims=True)
        h = h * lax.rsqrt(ms + NORM_EPS) * g_ref[...]
    o_ref[0] = h


def _combine(h1, y, tg, g, final_norm):
    batch, t, d = h1.shape
    tt = t // 6 if t % 48 == 0 else t
    per_batch = t // tt
    per_pick = batch * per_batch
    blk = lambda n: pl.BlockSpec((1, tt, n), lambda b, j: (b, j, 0))
    pick = lambda kk: pl.BlockSpec((tt * (d // LANES), LANES), lambda b, j: (kk * per_pick + b * per_batch + j, 0))
    return pl.pallas_call(
        functools.partial(_combine_kernel, final_norm=final_norm),
        grid=(batch, per_batch),
        in_specs=[blk(d)] + [pick(kk) for kk in range(TOP_K)] + [blk(LANES), pl.BlockSpec((1, d), lambda b, j: (0, 0))],
        out_specs=blk(d),
        out_shape=jax.ShapeDtypeStruct((batch, t, d), F32),
        compiler_params=_cparams(("parallel", "parallel")),
        name="combine",
    )(h1, y, y, y, y, tg, g)


def _pair_blockdiag(w):
    h, a, b = w.shape
    w = w.reshape(h // 2, 2, a, b)
    z = jnp.zeros_like(w[:, 0])
    top = jnp.concatenate([w[:, 0], z], axis=2)
    bot = jnp.concatenate([z, w[:, 1]], axis=2)
    return jnp.concatenate([top, bot], axis=1)


def _uk_per_head(w_uk):
    h, a, b = w_uk.shape
    z = jnp.zeros_like(w_uk)
    even = jnp.concatenate([w_uk, z], axis=1)
    odd = jnp.concatenate([z, w_uk], axis=1)
    sel = (jnp.arange(h) % 2 == 0)[:, None, None]
    return jnp.where(sel, even, odd)


def kernel(x, meta_tokens, norm_mix_g, w_in, b_forget, q_norm_g, w_uq, kv_norm_g, w_uk, w_uv, w_qidx,
           kidx_ln_g, kidx_ln_b, w_branch_a, w_branch_b, w_out, norm_ffn_g, w_router, b_router, w_gate_up,
           b_gate_up, w_down, b_down, norm_final_g):
    batch, seq, d = x.shape
    t = seq + N_META
    tp = -(-t // Q_BLOCK) * Q_BLOCK
    k_sel = min(IDX_TOPK_MAX, seq // 4)
    depth = w_in.shape[0]
    h = jnp.concatenate([jnp.broadcast_to(meta_tokens.astype(x.dtype)[None], (batch, N_META, d)), x], axis=1)
    for l in range(depth):
        hp = jnp.pad(h, ((0, 0), (0, tp - t), (0, 0)))
        rows = batch * tp
        q, k, v, cq, small3, sga, sgb = _in_proj(
            hp.reshape(rows, d), norm_mix_g[l].astype(F32)[None], _layout_w_in(w_in[l]))
        fc = _forget_cumsum(small3, b_forget[l], batch, tp)
        fr = jnp.transpose(fc[:, :, FA_LO:FA_LO + FOX_HEADS], (0, 2, 1)).reshape(batch, FOX_HEADS // 2, 2, tp)
        sh = lambda a: a.reshape(batch, tp, a.shape[-1])
        kt = jnp.transpose(k.reshape(batch, tp, FOX_HEADS // 2, LANES), (0, 2, 3, 1))
        o_a = _fox_attention(sh(q), kt, sh(v), fr)
        qq, qidx, kk = _dsa_prep(cq, small3, tp, q_norm_g[l], kv_norm_g[l], kidx_ln_g[l], kidx_ln_b[l],
                                 _layout_w_q(w_uq[l], w_qidx[l]), _uk_per_head(w_uk[l]).astype(BF16))
        o_b = _dsa_attention(sh(qq), sh(qidx), sh(small3), sh(kk), _pair_blockdiag(w_uv[l]).astype(BF16), t, k_sel)
        w_r = jnp.pad(w_router[l], ((0, 0), (0, LANES - N_EXPERTS))).astype(BF16)
        b_r = jnp.pad(b_router[l].astype(F32), (0, LANES - N_EXPERTS), constant_values=NEG_INF)[None]
        h1, hn, ti, tg = _merge_router(
            o_a, o_b, sh(sga), sh(sgb), hp, t, w_branch_a[l].astype(BF16), w_branch_b[l].astype(BF16),
            w_out[l].astype(BF16), norm_ffn_g[l].astype(F32)[None], w_r, b_r)
        n_tok = batch * t
        block_expert, slot_tok, slot_dst = _routing_tables(ti[:, :, :TOP_K], n_tok)
        y = _experts(hn.reshape(n_tok * (d // LANES), LANES), block_expert, slot_tok, slot_dst,
                     w_gate_up[l].astype(BF16), b_gate_up[l].astype(F32)[:, None, :],
                     w_down[l].astype(BF16), b_down[l].astype(F32)[:, None, :])
        h = _combine(h1, y, tg, norm_final_g.astype(F32)[None], final_norm=(l == depth - 1))
    return h[:, N_META:]
```

```python
import functools

import numpy as np
import jax
import jax.numpy as jnp
from jax import lax
from jax.experimental import pallas as pl
from jax.experimental.pallas import tpu as pltpu

F32 = jnp.float32
BF16 = jnp.bfloat16
I32 = jnp.int32

D_MODEL = 1024
CHUNK = 64
N_META = 16
Q_BLOCK = 128
ROPE_THETA = 10000.0
NORM_EPS = 1e-6
NEG_INF = -1e30

FOX_HEADS = 8
FOX_HEAD_DIM = 64
DSA_HEADS = 8
Q_RANK = 256
KV_RANK = 128
QK_NOPE_DIM = 64
QK_ROPE_DIM = 32
V_HEAD_DIM = 64
IDX_HEADS = 4
IDX_HEAD_DIM = 64
IDX_ROPE_DIM = 32
IDX_TOPK_MAX = 256
N_EXPERTS = 32
TOP_K = 4
D_EXPERT = D_MODEL
SWIGLU_ALPHA = 1.702
SWIGLU_LIMIT = 7.0

FOX_WIDTH = FOX_HEADS * FOX_HEAD_DIM
DSA_WIDTH = DSA_HEADS * V_HEAD_DIM
IN_SPLITS = (FOX_WIDTH, FOX_WIDTH, FOX_WIDTH, FOX_HEADS, Q_RANK, KV_RANK, QK_ROPE_DIM,
             IDX_HEAD_DIM, IDX_HEADS, D_MODEL, D_MODEL)

LANES = 128
VMEM_LIMIT = 56 * 1024 * 1024
KEY_GROUP = 512
KEY_TILE = 256
Q_ROWS = 256
LOG2E = 1.4426950408889634
EXPERT_ROWS = 256
EXPERT_CHUNKS = 4

KR_LO, KIDX_LO = 0, 32
FA_LO, IW_LO = 64, 72

C_Q, C_K, C_V = 0, 512, 1024
C_CQ = 1536
C_G1, C_G2, C_G3 = 1792, 1920, 2048
C_GA, C_GB = 2176, 3200
C_TOTAL = 4224


def _cparams(sem):
    return pltpu.CompilerParams(dimension_semantics=sem, vmem_limit_bytes=VMEM_LIMIT)


def _row_tile(rows, cap):
    t = cap
    while rows % t:
        t //= 2
    return t


def _rot_cols(w):
    h = w.shape[-1] // 2
    return jnp.concatenate([-w[..., h:], w[..., :h]], axis=-1)


def _perm_cols(w):
    h = w.shape[-1] // 2
    return jnp.concatenate([w[..., h:], w[..., :h]], axis=-1)


def _layout_w_in(w_in):
    offs = np.cumsum((0,) + IN_SPLITS)
    qa, ka, va, fa, cq, ckv, kr, kidx, iw, ga, gb = [w_in[:, offs[i]:offs[i + 1]] for i in range(11)]
    d = w_in.shape[0]
    z = lambda n: jnp.zeros((d, n), w_in.dtype)
    g2 = jnp.concatenate([kr, kidx, z(32)], axis=1)
    g3 = jnp.concatenate([_rot_cols(kr), _perm_cols(kidx[:, :IDX_ROPE_DIM]), fa, iw, z(52)], axis=1)
    w = jnp.concatenate([qa, ka, va, cq, ckv, g2, g3, ga, gb], axis=1)
    assert w.shape[1] == C_TOTAL
    return w.astype(BF16)


def _layout_w_q(w_uq, w_qidx):
    r = w_uq.shape[0]
    wq = w_uq.reshape(r, DSA_HEADS, QK_NOPE_DIM + QK_ROPE_DIM)
    nope = wq[:, :, :QK_NOPE_DIM].reshape(r, DSA_HEADS * QK_NOPE_DIM)
    rope = wq[:, :, QK_NOPE_DIM:]
    pad_r = lambda a: jnp.pad(a, ((0, 0), (0, 0), (KR_LO, LANES - KR_LO - QK_ROPE_DIM))).reshape(r, -1)
    wi = w_qidx.reshape(r, IDX_HEADS, IDX_HEAD_DIM)
    wi_rot = jnp.concatenate([_rot_cols(wi[:, :, :IDX_ROPE_DIM]), jnp.zeros_like(wi[:, :, IDX_ROPE_DIM:])], axis=-1)
    pad_i = lambda a: jnp.pad(a, ((0, 0), (0, 0), (KIDX_LO, LANES - KIDX_LO - IDX_HEAD_DIM))).reshape(r, -1)
    w = jnp.concatenate([nope, pad_r(rope), pad_r(_rot_cols(rope)), pad_i(wi), pad_i(wi_rot)], axis=1)
    return w.astype(BF16)


def _rope_tables(tp):
    half = QK_ROPE_DIM // 2
    inv = ROPE_THETA ** (-jnp.arange(half, dtype=F32) / half)
    ang = jnp.arange(tp, dtype=F32)[:, None] * inv[None, :]
    cos, sin = jnp.cos(ang), jnp.sin(ang)
    cos32 = jnp.concatenate([cos, cos], axis=1)
    sin32 = jnp.concatenate([sin, sin], axis=1)
    one, zero = jnp.ones((tp, 1), F32), jnp.zeros((tp, 1), F32)
    cos_k = jnp.concatenate([cos32, cos32, jnp.tile(one, (1, 64))], axis=1)
    sin_k = jnp.concatenate([sin32, sin32, jnp.tile(zero, (1, 64))], axis=1)
    return cos_k, sin_k


def _inproj_kernel(x_ref, g_ref, w_ref, q_ref, k_ref, v_ref, cq_ref, sm_ref, ga_ref, gb_ref):
    x = x_ref[...]
    ms = jnp.mean(x * x, axis=-1, keepdims=True)
    xn = (x * lax.rsqrt(ms + NORM_EPS) * g_ref[...]).astype(BF16)

    def mm(lo, hi):
        return jnp.dot(xn, w_ref[:, lo:hi], preferred_element_type=F32)

    q_ref[...] = (mm(C_Q, C_K) * (FOX_HEAD_DIM ** -0.5 * LOG2E)).astype(BF16)
    k_ref[...] = mm(C_K, C_V).astype(BF16)
    lane = lax.broadcasted_iota(I32, (x.shape[0], LANES), 1)
    ones_lane = jnp.where(lane == 0, 1.0, 0.0).astype(BF16)
    for p in range(FOX_HEADS // 2):
        v_ref[:, 2 * p * LANES:(2 * p + 1) * LANES] = mm(C_V + p * LANES, C_V + (p + 1) * LANES).astype(BF16)
        v_ref[:, (2 * p + 1) * LANES:(2 * p + 2) * LANES] = ones_lane
    cq_ref[...] = mm(C_CQ, C_G1)
    sm_ref[...] = mm(C_G1, C_GA)
    ga_ref[...] = (1.0 / (1.0 + jnp.exp(-mm(C_GA, C_GB)))).astype(BF16)
    gb_ref[...] = (1.0 / (1.0 + jnp.exp(-mm(C_GB, C_TOTAL)))).astype(BF16)


def _in_proj(hp, g, w_cat):
    rows, d = hp.shape
    tm = _row_tile(rows, 512)
    row = lambda n: pl.BlockSpec((tm, n), lambda i: (i, 0))
    full = lambda a: pl.BlockSpec(a.shape, lambda i: (0,) * a.ndim)
    outs = [(FOX_WIDTH, BF16), (FOX_WIDTH, BF16), (2 * FOX_WIDTH, BF16), (Q_RANK, F32), (3 * LANES, F32),
            (D_MODEL, BF16), (D_MODEL, BF16)]
    return pl.pallas_call(
        _inproj_kernel,
        grid=(rows // tm,),
        in_specs=[row(d), full(g), full(w_cat)],
        out_specs=[row(n) for n, _ in outs],
        out_shape=[jax.ShapeDtypeStruct((rows, n), dt) for n, dt in outs],
        compiler_params=_cparams(("parallel",)),
        name="in_proj",
    )(hp, g, w_cat)


def _split3(x):
    hi = x.astype(BF16)
    r1 = x - hi.astype(F32)
    mid = r1.astype(BF16)
    lo = (r1 - mid.astype(F32)).astype(BF16)
    return hi, mid, lo


def _fcum_kernel(sm_ref, b_ref, tri_ref, o_ref):
    tp = o_ref.shape[1]
    tri = tri_ref[...]
    carry = jnp.zeros((1, LANES), F32)
    for j in range(tp // LANES):
        z = sm_ref[0, j * LANES:(j + 1) * LANES, :] + b_ref[...]
        ls = jnp.minimum(z, 0.0) - jnp.log(1.0 + jnp.exp(-jnp.abs(z)))
        c = carry
        for part in _split3(ls):
            c = c + jnp.dot(tri, part, preferred_element_type=F32)
        o_ref[0, j * LANES:(j + 1) * LANES, :] = c * LOG2E
        carry = c[LANES - 1:LANES, :]


def _forget_cumsum(small3, b_forget, batch, tp):
    g3 = small3.reshape(batch, tp, 3 * LANES)
    bias = jnp.zeros((1, LANES), F32).at[0, FA_LO:FA_LO + FOX_HEADS].set(b_forget.astype(F32))
    tri = jnp.tril(jnp.ones((LANES, LANES), F32)).astype(BF16)
    return pl.pallas_call(
        _fcum_kernel,
        grid=(batch,),
        in_specs=[pl.BlockSpec((1, tp, LANES), lambda b: (b, 0, 2)),
                  pl.BlockSpec((1, LANES), lambda b: (0, 0)),
                  pl.BlockSpec((LANES, LANES), lambda b: (0, 0))],
        out_specs=pl.BlockSpec((1, tp, LANES), lambda b: (b, 0, 0)),
        out_shape=jax.ShapeDtypeStruct((batch, tp, LANES), F32),
        compiler_params=_cparams(("parallel",)),
        name="fcum",
    )(g3, bias, tri)


def _key_widths(tp):
    ws = list(range(KEY_GROUP, tp, KEY_GROUP)) + [tp]
    return ws


def _query_steps(tp):
    return -(-tp // Q_ROWS)


def _for_key_width(qi, tp, body, q_rows=Q_ROWS, lookahead=0):
    groups = []
    for step in range(-(-tp // q_rows)):
        rows = min(q_rows, tp - step * q_rows)
        need = min(tp, step * q_rows + rows + lookahead)
        w = min(x for x in _key_widths(tp) if x >= need)
        if groups and groups[-1][:2] == [rows, w]:
            groups[-1][3] = step
        else:
            groups.append([rows, w, step, step])
    for rows, w, first, last in groups:
        pl.when((qi >= first) & (qi <= last))(functools.partial(body, rows, w, first))


def _key_tiles(w):
    return [(k0, min(KEY_TILE, w - k0)) for k0 in range(0, w, KEY_TILE)]


def _lane_group_max(mx, s):
    for c in range(s.shape[1] // LANES):
        mx = jnp.maximum(mx, s[:, c * LANES:(c + 1) * LANES])
    return mx


def _fox_kernel(q_ref, kt_ref, v_ref, fr_ref, o_ref, s_ref):
    qi = pl.program_id(2)
    tp = kt_ref.shape[3]

    def body(rows, w, first):
        q = q_ref[0, :rows]
        lane = lax.broadcasted_iota(I32, (rows, LANES), 1)
        zero = jnp.zeros_like(q)
        qs = jnp.concatenate([jnp.where(lane < FOX_HEAD_DIM, q, zero),
                              jnp.where(lane >= FOX_HEAD_DIM, q, zero)], axis=0)
        mxs = [jnp.full((rows, LANES), -jnp.inf, F32)] * 2
        for k0, tk in _key_tiles(w):
            s2 = jnp.dot(qs, kt_ref[0, 0, :, k0:k0 + tk], preferred_element_type=F32)
            for hh in range(2):
                s = s2[hh * rows:(hh + 1) * rows] - fr_ref[0, 0, hh:hh + 1, k0:k0 + tk]
                if k0 + tk > first * Q_ROWS:
                    qpos = qi * Q_ROWS + lax.broadcasted_iota(I32, (rows, tk), 0)
                    kpos = k0 + lax.broadcasted_iota(I32, (rows, tk), 1)
                    s = jnp.where(kpos <= qpos, s, NEG_INF)
                s_ref[hh, :rows, k0:k0 + tk] = s
                mxs[hh] = _lane_group_max(mxs[hh], s)
        ms = [jnp.max(mx, axis=-1, keepdims=True) for mx in mxs]
        acc = jnp.zeros((2 * rows, 2 * LANES), F32)
        for k0, tk in _key_tiles(w):
            p2 = jnp.concatenate([jnp.exp2((s_ref[hh, :rows, k0:k0 + tk] - ms[hh]).astype(BF16))
                                  for hh in range(2)], axis=0)
            acc = acc + jnp.dot(p2, v_ref[0, k0:k0 + tk, :], preferred_element_type=F32)
        o = acc[:, :LANES] * (1.0 / acc[:, LANES:LANES + 1])
        o_ref[0, :rows] = jnp.where(lane < FOX_HEAD_DIM, o[:rows], o[rows:]).astype(BF16)

    _for_key_width(qi, tp, body)


def _fox_attention(q, kt, v, fr):
    batch, tp, _ = q.shape
    nq = _query_steps(tp)
    pairs = FOX_HEADS // 2
    return pl.pallas_call(
        _fox_kernel,
        grid=(batch, pairs, nq),
        in_specs=[pl.BlockSpec((1, Q_ROWS, LANES), lambda b, p, i: (b, i, p)),
                  pl.BlockSpec((1, 1, LANES, tp), lambda b, p, i: (b, p, 0, 0)),
                  pl.BlockSpec((1, tp, 2 * LANES), lambda b, p, i: (b, 0, p)),
                  pl.BlockSpec((1, 1, 2, tp), lambda b, p, i: (b, p, 0, 0))],
        out_specs=pl.BlockSpec((1, Q_ROWS, LANES), lambda b, p, i: (b, i, p)),
        out_shape=jax.ShapeDtypeStruct((batch, tp, FOX_WIDTH), BF16),
        scratch_shapes=[pltpu.VMEM((2, Q_ROWS, tp), F32)],
        compiler_params=_cparams(("parallel", "parallel", "arbitrary")),
        name="fox",
    )(q, kt, v, fr)


def _dsa_prep_kernel(cq_ref, sm_ref, qg_ref, kvg_ref, lng_ref, lnb_ref, lngp_ref, lnbp_ref,
                     cosk_ref, sink_ref, wq_ref, wuk_ref, qq_ref, qidx_ref, kk_ref):
    cq = cq_ref[...]
    ms = jnp.mean(cq * cq, axis=-1, keepdims=True)
    cqn = (cq * lax.rsqrt(ms + NORM_EPS) * qg_ref[...]).astype(BF16)
    cos = cosk_ref[...]
    sin = sink_ref[...]

    def mm(lo, hi):
        return jnp.dot(cqn, wq_ref[:, lo:hi], preferred_element_type=F32)

    nope_w = DSA_HEADS * QK_NOPE_DIM
    hw = DSA_HEADS * LANES
    iw = IDX_HEADS * LANES
    q_nope = mm(0, nope_w).astype(BF16)
    for h in range(DSA_HEADS):
        base = nope_w + h * LANES
        q_rope = mm(base, base + LANES) * cos + mm(base + hw, base + hw + LANES) * sin
        p = h // 2
        q_lat = jnp.dot(q_nope[:, p * LANES:(p + 1) * LANES], wuk_ref[h], preferred_element_type=F32)
        qscale = (QK_NOPE_DIM + QK_ROPE_DIM) ** -0.5 * LOG2E
        qq_ref[:, h * 2 * LANES:h * 2 * LANES + LANES] = (q_lat * qscale).astype(BF16)
        qq_ref[:, h * 2 * LANES + LANES:(h + 1) * 2 * LANES] = (q_rope * qscale).astype(BF16)
    for h in range(IDX_HEADS):
        base = nope_w + 2 * hw + h * LANES
        qi = mm(base, base + LANES) * cos + mm(base + iw, base + iw + LANES) * sin
        qidx_ref[:, h * LANES:(h + 1) * LANES] = qi.astype(BF16)

    ckv = sm_ref[:, 0:LANES]
    ms = jnp.mean(ckv * ckv, axis=-1, keepdims=True)
    kk_ref[:, 0:LANES] = (ckv * lax.rsqrt(ms + NORM_EPS) * kvg_ref[...]).astype(BF16)
    g2 = sm_ref[:, LANES:2 * LANES]
    g3 = sm_ref[:, 2 * LANES:3 * LANES]
    lane = lax.broadcasted_iota(I32, g2.shape, 1)
    is_idx = (lane >= KIDX_LO) & (lane < KIDX_LO + IDX_HEAD_DIM)
    is_rope = lane < KIDX_LO
    xi = jnp.where(is_idx, g2, 0.0)
    mu = jnp.sum(xi, axis=-1, keepdims=True) * (1.0 / IDX_HEAD_DIM)
    xc = jnp.where(is_idx, g2 - mu, 0.0)
    var = jnp.sum(xc * xc, axis=-1, keepdims=True) * (1.0 / IDX_HEAD_DIM)
    rstd = lax.rsqrt(var + NORM_EPS)
    y = xc * rstd * lng_ref[...] + lnb_ref[...]
    yp = (g3 - mu) * rstd * lngp_ref[...] + lnbp_ref[...]
    a = jnp.where(is_rope, g2, y)
    b = jnp.where(is_rope, g3, jnp.where(lane < KIDX_LO + IDX_ROPE_DIM, yp, 0.0))
    kx = jnp.where(lane == LANES - 1, 1.0, a * cos + b * sin)
    kk_ref[:, LANES:2 * LANES] = kx.astype(BF16)


def _dsa_prep(cq, small3, tp, q_norm_g, kv_norm_g, ln_g, ln_b, w_q, w_uk_pair):
    rows = cq.shape[0]
    tm = tp // 4 if tp % 64 == 0 else tp
    per = tp // tm
    cos_k, sin_k = _rope_tables(tp)
    lane_vec = lambda v, lo: jnp.zeros((1, LANES), F32).at[0, lo:lo + v.shape[0]].set(v.astype(F32))
    half = IDX_ROPE_DIM // 2
    sign = jnp.concatenate([-jnp.ones((half,), F32), jnp.ones((half,), F32)])
    g_perm = jnp.concatenate([ln_g[half:IDX_ROPE_DIM], ln_g[:half]]).astype(F32) * sign
    b_perm = jnp.concatenate([ln_b[half:IDX_ROPE_DIM], ln_b[:half]]).astype(F32) * sign
    row = lambda n: pl.BlockSpec((tm, n), lambda i: (i, 0))
    tab = pl.BlockSpec((tm, LANES), lambda i: (i % per, 0))
    full = lambda a: pl.BlockSpec(a.shape, lambda i: (0,) * a.ndim)
    vecs = [q_norm_g.astype(F32)[None], kv_norm_g.astype(F32)[None], lane_vec(ln_g, KIDX_LO),
            lane_vec(ln_b, KIDX_LO), lane_vec(g_perm, KIDX_LO), lane_vec(b_perm, KIDX_LO)]
    outs = [(DSA_HEADS * 2 * LANES, BF16), (IDX_HEADS * LANES, BF16), (2 * LANES, BF16)]
    return pl.pallas_call(
        _dsa_prep_kernel,
        grid=(rows // tm,),
        in_specs=[row(Q_RANK), row(3 * LANES)] + [full(v) for v in vecs] + [tab, tab, full(w_q), full(w_uk_pair)],
        out_specs=[row(n) for n, _ in outs],
        out_shape=[jax.ShapeDtypeStruct((rows, n), dt) for n, dt in outs],
        compiler_params=_cparams(("parallel",)),
        name="dsa_prep",
    )(cq, small3, *vecs, cos_k, sin_k, w_q, w_uk_pair)


def _dsa_tiled_kernel(qq_ref, qidx_ref, sm_ref, kv_ref, kt_ref, tri_ref, wuv_ref, o_ref,
                      key_ref, bias_ref, s_ref, mx_ref, acc_ref, *, tp, n_valid, k_sel):
    qi = pl.program_id(1)
    n_tiles = kv_ref.shape[1]
    int_min = jnp.int32(-2 ** 31)
    chain = Q_BLOCK
    nq = _query_steps(tp)
    kf = jnp.float32(k_sel)

    def body(rows):
        nt = jnp.minimum((qi * Q_ROWS + rows + Q_BLOCK + KEY_TILE - 1) // KEY_TILE, n_tiles)
        starts = tuple(range(0, rows, chain))

        def index_tile(t, c):
            qpos = qi * Q_ROWS + lax.broadcasted_iota(I32, (rows, 1), 0)
            kend = (((qpos + (CHUNK - N_META)) // CHUNK) + 1) * CHUNK - (CHUNK - N_META)
            kend = jnp.minimum(kend, n_valid)
            qidx_all = jnp.concatenate(
                [qidx_ref[0, :rows, h * LANES:(h + 1) * LANES] for h in range(IDX_HEADS)], axis=0)
            d_all = jnp.dot(qidx_all, kt_ref[0, t, LANES:, :], preferred_element_type=F32)
            acc = None
            for h in range(IDX_HEADS):
                wh = sm_ref[0, :rows, IW_LO + h:IW_LO + h + 1] * (IDX_HEADS ** -0.5)
                term = wh * jnp.maximum(d_all[h * rows:(h + 1) * rows], 0.0)
                acc = term if acc is None else acc + term
            isc = acc * (IDX_HEAD_DIM ** -0.5)
            isc = jnp.where(isc == 0.0, 0.0, isc)
            adm = (t * KEY_TILE + lax.broadcasted_iota(I32, (rows, KEY_TILE), 1)) < kend
            bits = pltpu.bitcast(isc, I32)
            key = bits ^ ((bits >> 31) & jnp.int32(0x7FFFFFFF))
            key_ref[t, :rows] = jnp.where(adm, key, int_min)
            bias_ref[t, :rows] = jnp.where(adm & (isc > 0.5 * NEG_INF), 0.0, NEG_INF)
            return c

        lax.fori_loop(0, nt, index_tile, 0)

        def counts(pred, thrs):
            def tile(t, parts):
                out = []
                for part, thr_c, r0 in zip(parts, thrs, starts):
                    hit = jnp.where(pred(key_ref[t, r0:r0 + chain], thr_c), 1.0, 0.0)
                    for c in range(KEY_TILE // LANES):
                        part = part + hit[:, c * LANES:(c + 1) * LANES]
                    out.append(part)
                return tuple(out)
            parts = lax.fori_loop(0, nt, tile, tuple(jnp.zeros((chain, LANES), F32) for _ in starts))
            return [jnp.sum(part, axis=-1, keepdims=True) for part in parts]

        ge = lambda k, c: k >= c
        zero = jnp.zeros((chain, 1), I32)
        t0 = tuple(jnp.where(n >= kf, jnp.int32(0), int_min) for n in counts(ge, [zero] * len(starts)))

        def bit_step(i, ts):
            bit = jnp.left_shift(jnp.int32(1), 30 - i)
            cands = [t | bit for t in ts]
            return tuple(jnp.where(n >= kf, c, t) for n, c, t in zip(counts(ge, cands), cands, ts))

        thrs = lax.fori_loop(0, 31, bit_step, t0)
        n_gts = counts(lambda k, c: k > c, thrs)
        thr = jnp.concatenate(thrs, axis=0) if len(thrs) > 1 else thrs[0]
        need = kf - (jnp.concatenate(n_gts, axis=0) if len(n_gts) > 1 else n_gts[0])

        def tie_tile(t, run):
            for c in range(KEY_TILE // LANES):
                sl = slice(c * LANES, (c + 1) * LANES)
                key = key_ref[t, :rows, sl]
                eq = key == thr
                cnt = jnp.dot(jnp.where(eq, 1.0, 0.0).astype(BF16), tri_ref[...], preferred_element_type=F32)
                take = (key > thr) | (eq & (cnt[:, :LANES] + run <= need))
                bias_ref[t, :rows, sl] = jnp.where(take, bias_ref[t, :rows, sl], NEG_INF)
                run = run + cnt[:, LANES:]
            return run

        lax.fori_loop(0, nt, tie_tile, jnp.zeros((rows, LANES), F32))

        for p in range(DSA_HEADS // 2):
            def stacked_q():
                return jnp.concatenate(
                    [qq_ref[0, :rows, h * 2 * LANES:(h + 1) * 2 * LANES] for h in (2 * p, 2 * p + 1)], axis=0)

            mx_ref[:, :rows] = jnp.full((2, rows, LANES), -jnp.inf, F32)
            acc_ref[:2 * rows] = jnp.zeros((2 * rows, 2 * LANES), F32)

            def score_tile(t, c):
                s2 = jnp.dot(stacked_q(), kt_ref[0, t], preferred_element_type=F32)
                for hh in range(2):
                    s = s2[hh * rows:(hh + 1) * rows] + bias_ref[t, :rows]
                    s_ref[hh, t, :rows] = s
                    mx_ref[hh, :rows] = _lane_group_max(mx_ref[hh, :rows], s)
                return c

            lax.fori_loop(0, nt, score_tile, 0)
            for hh in range(2):
                m = jnp.max(mx_ref[hh, :rows], axis=-1, keepdims=True)
                mx_ref[hh, :rows] = jnp.broadcast_to(m, (rows, LANES))

            def value_tile(t, c):
                p2 = []
                for hh in range(2):
                    m = mx_ref[hh, :rows]
                    m = jnp.concatenate([m] * (KEY_TILE // LANES), axis=1)
                    p2.append(jnp.exp2((s_ref[hh, t, :rows] - m).astype(BF16)))
                acc_ref[:2 * rows] += jnp.dot(jnp.concatenate(p2, axis=0), kv_ref[0, t], preferred_element_type=F32)
                return c

            lax.fori_loop(0, nt, value_tile, 0)
            acc = acc_ref[:2 * rows]
            lat = (acc[:, :LANES] * (1.0 / acc[:, 2 * LANES - 1:2 * LANES])).astype(BF16)
            o_lat = jnp.concatenate([lat[:rows], lat[rows:]], axis=1)
            o_ref[0, :rows, p * LANES:(p + 1) * LANES] = jnp.dot(
                o_lat, wuv_ref[p], preferred_element_type=F32).astype(BF16)

    last_rows = tp - (nq - 1) * Q_ROWS
    if last_rows == Q_ROWS:
        body(Q_ROWS)
    else:
        pl.when(qi < nq - 1)(functools.partial(body, Q_ROWS))
        pl.when(qi == nq - 1)(functools.partial(body, last_rows))


def _dsa_attention_tiled(qq, qidx, small3, kk, w_uv_pair, n_valid, k_sel):
    batch, tp, width = kk.shape
    nq = _query_steps(tp)
    n_tiles = -(-tp // KEY_TILE)
    kv = jnp.pad(kk, ((0, 0), (0, n_tiles * KEY_TILE - tp), (0, 0))).reshape(batch, n_tiles, KEY_TILE, width)
    kt = jnp.transpose(kv, (0, 1, 3, 2))
    ones = jnp.ones((LANES, LANES), F32)
    tri = jnp.concatenate([jnp.triu(ones), ones], axis=1).astype(BF16)
    qblk = lambda n: pl.BlockSpec((1, Q_ROWS, n), lambda b, i: (b, i, 0))
    tiles = lambda a: pl.BlockSpec((1,) + a.shape[1:], lambda b, i: (b, 0, 0, 0))
    return pl.pallas_call(
        functools.partial(_dsa_tiled_kernel, tp=tp, n_valid=n_valid, k_sel=k_sel),
        grid=(batch, nq),
        in_specs=[qblk(DSA_HEADS * 2 * LANES), qblk(IDX_HEADS * LANES),
                  pl.BlockSpec((1, Q_ROWS, LANES), lambda b, i: (b, i, 2)),
                  tiles(kv), tiles(kt),
                  pl.BlockSpec((LANES, 2 * LANES), lambda b, i: (0, 0)),
                  pl.BlockSpec(w_uv_pair.shape, lambda b, i: (0, 0, 0))],
        out_specs=qblk(DSA_WIDTH),
        out_shape=jax.ShapeDtypeStruct((batch, tp, DSA_WIDTH), BF16),
        scratch_shapes=[pltpu.VMEM((n_tiles, Q_ROWS, KEY_TILE), I32),
                        pltpu.VMEM((n_tiles, Q_ROWS, KEY_TILE), F32),
                        pltpu.VMEM((2, n_tiles, Q_ROWS, KEY_TILE), F32),
                        pltpu.VMEM((2, Q_ROWS, LANES), F32),
                        pltpu.VMEM((2 * Q_ROWS, 2 * LANES), F32)],
        compiler_params=_cparams(("parallel", "arbitrary")),
        name="dsa",
    )(qq, qidx, small3, kv, kt, tri, w_uv_pair)


def _dsa_kernel(qq_ref, qidx_ref, sm_ref, kk_ref, tri_ref, wuv_ref, o_ref, key_ref, bias_ref, *,
                n_valid, k_sel):
    qi = pl.program_id(1)
    tp = kk_ref.shape[1]
    int_min = jnp.int32(-2 ** 31)

    def body(rows, w, first):
        del first
        kx = kk_ref[0, :w, LANES:]
        acc = jnp.zeros((rows, w), F32)
        for h in range(IDX_HEADS):
            d = lax.dot_general(qidx_ref[0, :, h * LANES:(h + 1) * LANES], kx,
                                (((1,), (1,)), ((), ())), preferred_element_type=F32)
            wh = sm_ref[0, :, IW_LO + h:IW_LO + h + 1] * (IDX_HEADS ** -0.5)
            acc = acc + wh * jnp.maximum(d, 0.0)
        isc = acc * (IDX_HEAD_DIM ** -0.5)
        isc = jnp.where(isc == 0.0, 0.0, isc)
        qpos = qi * rows + lax.broadcasted_iota(I32, (rows, 1), 0)
        kend = (((qpos + (CHUNK - N_META)) // CHUNK) + 1) * CHUNK - (CHUNK - N_META)
        kend = jnp.minimum(kend, n_valid)
        kpos = lax.broadcasted_iota(I32, (rows, w), 1)
        adm = kpos < kend
        bits = pltpu.bitcast(isc, I32)
        key = bits ^ ((bits >> 31) & jnp.int32(0x7FFFFFFF))
        key = jnp.where(adm, key, int_min)
        key_ref[:, :w] = key
        ok_bias = jnp.where(adm & (isc > 0.5 * NEG_INF), 0.0, NEG_INF)

        def count_ge(c):
            return jnp.sum(jnp.where(key_ref[:, :w] >= c, 1.0, 0.0), axis=-1, keepdims=True)

        kf = jnp.float32(k_sel)
        t0 = jnp.where(count_ge(jnp.zeros((rows, 1), I32)) >= kf, jnp.int32(0), int_min)

        def bit_step(i, t):
            cand = t | jnp.left_shift(jnp.int32(1), 30 - i)
            return jnp.where(count_ge(cand) >= kf, cand, t)

        thr = lax.fori_loop(0, 31, bit_step, t0)

        gt = key > thr
        need = kf - jnp.sum(jnp.where(gt, 1.0, 0.0), axis=-1, keepdims=True)
        run = jnp.zeros((rows, 1), F32)
        tri = tri_ref[...]
        for j in range(w // LANES):
            sl = slice(j * LANES, (j + 1) * LANES)
            eq = key[:, sl] == thr
            e = jnp.where(eq, 1.0, 0.0)
            pre = jnp.dot(e.astype(BF16), tri, preferred_element_type=F32) + run
            take = gt[:, sl] | (eq & (pre <= need))
            bias_ref[:, sl] = jnp.where(take, ok_bias[:, sl], NEG_INF)
            run = pre[:, LANES - 1:LANES]

        bias = bias_ref[:, :w]
        kk = kk_ref[0, :w, :]
        kv = kk_ref[0, :w, :LANES]
        for p in range(DSA_HEADS // 2):
            lat = []
            for h in (2 * p, 2 * p + 1):
                qh = qq_ref[0, :, h * 2 * LANES:(h + 1) * 2 * LANES]
                s = lax.dot_general(qh, kk, (((1,), (1,)), ((), ())), preferred_element_type=F32) + bias
                m = jnp.max(s, axis=-1, keepdims=True)
                pr = jnp.exp2(s - m)
                l = jnp.sum(pr, axis=-1, keepdims=True)
                lat.append((jnp.dot(pr.astype(BF16), kv, preferred_element_type=F32) * (1.0 / l)).astype(BF16))
            o_lat = jnp.concatenate(lat, axis=1)
            o_ref[0, :, p * LANES:(p + 1) * LANES] = jnp.dot(
                o_lat, wuv_ref[p], preferred_element_type=F32).astype(BF16)

    _for_key_width(qi, tp, body, q_rows=Q_BLOCK, lookahead=Q_BLOCK)


def _dsa_attention(qq, qidx, small3, kk, w_uv_pair, n_valid, k_sel):
    batch, tp, _ = kk.shape
    nq = tp // Q_BLOCK
    tri = jnp.triu(jnp.ones((LANES, LANES), F32)).astype(BF16)
    qblk = lambda n: pl.BlockSpec((1, Q_BLOCK, n), lambda b, i: (b, i, 0))
    return pl.pallas_call(
        functools.partial(_dsa_kernel, n_valid=n_valid, k_sel=k_sel),
        grid=(batch, nq),
        in_specs=[qblk(DSA_HEADS * 2 * LANES), qblk(IDX_HEADS * LANES),
                  pl.BlockSpec((1, Q_BLOCK, LANES), lambda b, i: (b, i, 2)),
                  pl.BlockSpec((1, tp, 2 * LANES), lambda b, i: (b, 0, 0)),
                  pl.BlockSpec((LANES, LANES), lambda b, i: (0, 0)),
                  pl.BlockSpec(w_uv_pair.shape, lambda b, i: (0, 0, 0))],
        out_specs=qblk(DSA_WIDTH),
        out_shape=jax.ShapeDtypeStruct((batch, tp, DSA_WIDTH), BF16),
        scratch_shapes=[pltpu.VMEM((Q_BLOCK, tp), I32), pltpu.VMEM((Q_BLOCK, tp), F32)],
        compiler_params=_cparams(("parallel", "arbitrary")),
        name="dsa",
    )(qq, qidx, small3, kk, tri, w_uv_pair)


def _store_row_tiles(ref, x):
    rows, d = x.shape
    sub = d // LANES
    for j in range(sub):
        ref[pl.ds(j, rows, stride=sub), :] = x[:, j * LANES:(j + 1) * LANES]


def _load_row_tiles(ref, rows):
    sub = ref.shape[0] // rows
    return jnp.concatenate([ref[pl.ds(j, rows, stride=sub), :] for j in range(sub)], axis=1)


def _merge_kernel(oa_ref, ob_ref, ga_ref, gb_ref, h_ref, wa_ref, wb_ref, wo_ref, g_ref, wr_ref, br_ref,
                  h1_ref, hn_ref, ti_ref, tg_ref):
    a = jnp.dot(oa_ref[0], wa_ref[...], preferred_element_type=F32)
    b = jnp.dot(ob_ref[0], wb_ref[...], preferred_element_type=F32)
    merged = ga_ref[0].astype(F32) * a + gb_ref[0].astype(F32) * b
    mix = jnp.dot(merged.astype(BF16), wo_ref[...], preferred_element_type=F32)
    h1 = h_ref[0] + mix
    h1_ref[0] = h1
    ms = jnp.mean(h1 * h1, axis=-1, keepdims=True)
    hn = h1 * lax.rsqrt(ms + NORM_EPS) * g_ref[...]
    _store_row_tiles(hn_ref.at[0], hn)
    logits = jnp.dot(hn.astype(BF16), wr_ref[...], preferred_element_type=F32) + br_ref[...]
    lane = lax.broadcasted_iota(I32, logits.shape, 1)
    idx_out = jnp.zeros(logits.shape, I32)
    val_out = jnp.zeros(logits.shape, F32)
    vals = []
    for kk in range(TOP_K):
        m = jnp.max(logits, axis=-1, keepdims=True)
        idx = jnp.min(jnp.where(logits == m, lane, LANES), axis=-1, keepdims=True)
        logits = jnp.where(lane == idx, -jnp.inf, logits)
        idx_out = jnp.where(lane == kk, idx, idx_out)
        vals.append(m)
    es = [jnp.exp(v - vals[0]) for v in vals]
    den = es[0] + es[1] + es[2] + es[3]
    for kk in range(TOP_K):
        val_out = jnp.where(lane == kk, es[kk] / den, val_out)
    ti_ref[0] = idx_out
    tg_ref[0] = val_out


def _merge_router(o_a, o_b, sga, sgb, hp, t_valid, w_a, w_b, w_o, g_ffn, w_r, b_r):
    batch, tp, d = hp.shape
    tt = t_valid // 3 if t_valid % 48 == 0 else t_valid
    blk = lambda n: pl.BlockSpec((1, tt, n), lambda b, j: (b, j, 0))
    full = lambda a: pl.BlockSpec(a.shape, lambda b, j: (0,) * a.ndim)
    consts = [w_a, w_b, w_o, g_ffn, w_r, b_r]
    sub = d // LANES
    return pl.pallas_call(
        _merge_kernel,
        grid=(batch, t_valid // tt),
        in_specs=[blk(FOX_WIDTH), blk(DSA_WIDTH), blk(d), blk(d), blk(d)] + [full(c) for c in consts],
        out_specs=[blk(d), pl.BlockSpec((1, tt * sub, LANES), lambda b, j: (b, j, 0)), blk(LANES), blk(LANES)],
        out_shape=[jax.ShapeDtypeStruct((batch, t_valid, d), F32),
                   jax.ShapeDtypeStruct((batch, t_valid * sub, LANES), F32),
                   jax.ShapeDtypeStruct((batch, t_valid, LANES), I32),
                   jax.ShapeDtypeStruct((batch, t_valid, LANES), F32)],
        compiler_params=_cparams(("parallel", "parallel")),
        name="merge_router",
    )(o_a, o_b, sga, sgb, hp, *consts)


def _experts_kernel(be_ref, tok_ref, tokn_ref, dstp_ref, dst_ref, hn_hbm, wgu_ref, bgu_ref, wd_ref, bd_ref,
                    y_hbm, x0, x1, y0, y1, xb_ref, act_ref, gsem, ssem):
    i = pl.program_id(0)
    nb = pl.num_programs(0)
    sub = wd_ref.shape[2] // LANES
    rows = x0.shape[0] // sub
    xbufs, ybufs = (x0, x1), (y0, y1)

    def gather_row(idx_ref, s, r):
        src = pl.multiple_of(idx_ref[0, 0, r], sub)
        return pltpu.make_async_copy(hn_hbm.at[pl.ds(src, sub)], xbufs[s].at[pl.ds(r * sub, sub)], gsem.at[s])

    def scatter_row(idx_ref, s, r):
        dst = pl.multiple_of(idx_ref[0, 0, r], sub)
        return pltpu.make_async_copy(ybufs[s].at[pl.ds(r * sub, sub)], y_hbm.at[pl.ds(dst, sub)], ssem.at[s])

    def wait_gather(s):
        pltpu.make_async_copy(hn_hbm.at[pl.ds(0, rows * sub)], xbufs[s], gsem.at[s]).wait()

    def wait_scatter(s):
        pltpu.make_async_copy(ybufs[s], y_hbm.at[pl.ds(0, rows * sub)], ssem.at[s]).wait()

    @pl.when(i == 0)
    def _():
        y1[...] = jnp.zeros(y1.shape, y1.dtype)

        def issue(r, c):
            gather_row(tok_ref, 0, r).start()
            return c
        lax.fori_loop(0, rows, issue, 0)

    def step(slot):
        other = 1 - slot

        @pl.when(i >= 1)
        def _():
            wait_scatter(slot)

        wait_gather(slot)
        xb_ref[...] = _load_row_tiles(xbufs[slot], rows).astype(BF16)
        per = rows // EXPERT_CHUNKS
        cw = D_EXPERT // EXPERT_CHUNKS
        for c in range(EXPERT_CHUNKS):
            @pl.when(be_ref[i] >= 0)
            def _():
                for r in range(c * per, (c + 1) * per):
                    gather_row(tokn_ref, other, r).start()
                for r in range(c * per, (c + 1) * per):
                    scatter_row(dstp_ref, other, r).start()
                x = xb_ref[...]
                glu = jnp.dot(x, wgu_ref[0, :, c * cw:(c + 1) * cw], preferred_element_type=F32)
                glu = jnp.minimum(glu + bgu_ref[0, :, c * cw:(c + 1) * cw], SWIGLU_LIMIT)
                lo = D_EXPERT + c * cw
                lin = jnp.dot(x, wgu_ref[0, :, lo:lo + cw], preferred_element_type=F32)
                lin = jnp.clip(lin + bgu_ref[0, :, lo:lo + cw], -SWIGLU_LIMIT, SWIGLU_LIMIT)
                act = glu * (1.0 / (1.0 + jnp.exp(-SWIGLU_ALPHA * glu))) * (lin + 1.0)
                act_ref[:, c * cw:(c + 1) * cw] = act.astype(BF16)

        _store_row_tiles(ybufs[slot],
                         jnp.dot(act_ref[...], wd_ref[0], preferred_element_type=F32) + bd_ref[0])

        @pl.when(i == nb - 1)
        def _():
            def issue(r, c):
                scatter_row(dst_ref, slot, r).start()
                return c
            lax.fori_loop(0, rows, issue, 0)
            wait_scatter(slot)
            wait_scatter(other)
            wait_gather(other)

    pl.when(i % 2 == 0)(functools.partial(step, 0))
    pl.when(i % 2 == 1)(functools.partial(step, 1))


def _experts(hn, block_expert, slot_tok, slot_dst, w_gu, b_gu, w_d, b_d):
    d = w_d.shape[2]
    sub = d // LANES
    n_slots = slot_tok.shape[0]
    rows = EXPERT_ROWS
    nb = n_slots // rows
    tok3 = (slot_tok * sub).reshape(nb, 1, rows)
    dst3 = (slot_dst * sub).reshape(nb, 1, rows)
    smem = lambda f: pl.BlockSpec((1, 1, rows), f, memory_space=pltpu.SMEM)
    grid_spec = pltpu.PrefetchScalarGridSpec(
        num_scalar_prefetch=1,
        grid=(nb,),
        in_specs=[smem(lambda i, be: (i, 0, 0)),
                  smem(lambda i, be: (jnp.minimum(i + 1, nb - 1), 0, 0)),
                  smem(lambda i, be: (jnp.maximum(i - 1, 0), 0, 0)),
                  smem(lambda i, be: (i, 0, 0)),
                  pl.BlockSpec(memory_space=pl.ANY),
                  pl.BlockSpec((1, d, 2 * D_EXPERT), lambda i, be: (be[i], 0, 0)),
                  pl.BlockSpec((1, 1, 2 * D_EXPERT), lambda i, be: (be[i], 0, 0)),
                  pl.BlockSpec((1, D_EXPERT, d), lambda i, be: (be[i], 0, 0)),
                  pl.BlockSpec((1, 1, d), lambda i, be: (be[i], 0, 0))],
        out_specs=pl.BlockSpec(memory_space=pl.ANY),
        scratch_shapes=[pltpu.VMEM((rows * sub, LANES), F32)] * 4 +
                       [pltpu.VMEM((rows, d), BF16), pltpu.VMEM((rows, D_EXPERT), BF16)] +
                       [pltpu.SemaphoreType.DMA((2,)), pltpu.SemaphoreType.DMA((2,))],
    )
    return pl.pallas_call(
        _experts_kernel,
        grid_spec=grid_spec,
        out_shape=jax.ShapeDtypeStruct((n_slots * sub, LANES), F32),
        compiler_params=_cparams(("arbitrary",)),
        name="experts",
    )(block_expert, tok3, tok3, dst3, dst3, hn, w_gu, b_gu, w_d, b_d)


def _routing_tables(top_idx, n_tok):
    rows = EXPERT_ROWS
    n_asg = n_tok * TOP_K
    flat_e = top_idx.reshape(-1).astype(I32)
    keys = jnp.sort(flat_e * n_asg + jnp.arange(n_asg, dtype=I32))
    order = keys % n_asg
    counts = jnp.sum(flat_e[:, None] == jnp.arange(N_EXPERTS, dtype=I32)[None, :], axis=0).astype(I32)
    padded = (counts + rows - 1) // rows * rows
    start = jnp.cumsum(counts) - counts
    pend = jnp.cumsum(padded)
    pstart = pend - padded
    nb = -(-n_asg // rows) + N_EXPERTS
    n_slots = nb * rows
    block_start = jnp.arange(nb, dtype=I32) * rows
    block_expert = jnp.minimum(
        jnp.sum((pend[None, :] <= block_start[:, None]).astype(I32), axis=1), N_EXPERTS - 1).astype(I32)
    slot = jnp.arange(n_slots, dtype=I32)
    e = jnp.repeat(block_expert, rows)
    r = slot - pstart[e]
    valid = (r >= 0) & (r < counts[e])
    asg = order[jnp.clip(start[e] + r, 0, n_asg - 1)]
    slot_tok = jnp.where(valid, asg // TOP_K, 0).astype(I32)
    pad_rank = jnp.cumsum(jnp.logical_not(valid).astype(I32)) - 1
    slot_dst = jnp.where(valid, (asg % TOP_K) * n_tok + asg // TOP_K, n_asg + pad_rank).astype(I32)
    return block_expert, slot_tok, slot_dst


def _combine_kernel(h_ref, y0_ref, y1_ref, y2_ref, y3_ref, tg_ref, g_ref, o_ref, *, final_norm):
    h = h_ref[0]
    for kk, y_ref in enumerate((y0_ref, y1_ref, y2_ref, y3_ref)):
        h = h + _load_row_tiles(y_ref, h.shape[0]) * tg_ref[0, :, kk:kk + 1]
    if final_norm:
        ms = jnp.mean(h * h, axis=-1, keepdims=True)
        h = h * lax.rsqrt(ms + NORM_EPS) * g_ref[...]
    o_ref[0] = h


def _combine(h1, y, tg, g, final_norm):
    batch, t, d = h1.shape
    tt = t // 6 if t % 48 == 0 else t
    per_batch = t // tt
    per_pick = batch * per_batch
    blk = lambda n: pl.BlockSpec((1, tt, n), lambda b, j: (b, j, 0))
    pick = lambda kk: pl.BlockSpec((tt * (d // LANES), LANES), lambda b, j: (kk * per_pick + b * per_batch + j, 0))
    return pl.pallas_call(
        functools.partial(_combine_kernel, final_norm=final_norm),
        grid=(batch, per_batch),
        in_specs=[blk(d)] + [pick(kk) for kk in range(TOP_K)] + [blk(LANES), pl.BlockSpec((1, d), lambda b, j: (0, 0))],
        out_specs=blk(d),
        out_shape=jax.ShapeDtypeStruct((batch, t, d), F32),
        compiler_params=_cparams(("parallel", "parallel")),
        name="combine",
    )(h1, y, y, y, y, tg, g)


def _pair_blockdiag(w):
    h, a, b = w.shape
    w = w.reshape(h // 2, 2, a, b)
    z = jnp.zeros_like(w[:, 0])
    top = jnp.concatenate([w[:, 0], z], axis=2)
    bot = jnp.concatenate([z, w[:, 1]], axis=2)
    return jnp.concatenate([top, bot], axis=1)


def _uk_per_head(w_uk):
    h, a, b = w_uk.shape
    z = jnp.zeros_like(w_uk)
    even = jnp.concatenate([w_uk, z], axis=1)
    odd = jnp.concatenate([z, w_uk], axis=1)
    sel = (jnp.arange(h) % 2 == 0)[:, None, None]
    return jnp.where(sel, even, odd)


def kernel(x, meta_tokens, norm_mix_g, w_in, b_forget, q_norm_g, w_uq, kv_norm_g, w_uk, w_uv, w_qidx,
           kidx_ln_g, kidx_ln_b, w_branch_a, w_branch_b, w_out, norm_ffn_g, w_router, b_router, w_gate_up,
           b_gate_up, w_down, b_down, norm_final_g):
    batch, seq, d = x.shape
    t = seq + N_META
    tp = -(-t // Q_BLOCK) * Q_BLOCK
    k_sel = min(IDX_TOPK_MAX, seq // 4)
    depth = w_in.shape[0]
    h = jnp.concatenate([jnp.broadcast_to(meta_tokens.astype(x.dtype)[None], (batch, N_META, d)), x], axis=1)
    for l in range(depth):
        hp = jnp.pad(h, ((0, 0), (0, tp - t), (0, 0)))
        rows = batch * tp
        q, k, v, cq, small3, sga, sgb = _in_proj(
            hp.reshape(rows, d), norm_mix_g[l].astype(F32)[None], _layout_w_in(w_in[l]))
        fc = _forget_cumsum(small3, b_forget[l], batch, tp)
        fr = jnp.transpose(fc[:, :, FA_LO:FA_LO + FOX_HEADS], (0, 2, 1)).reshape(batch, FOX_HEADS // 2, 2, tp)
        sh = lambda a: a.reshape(batch, tp, a.shape[-1])
        kt = jnp.transpose(k.reshape(batch, tp, FOX_HEADS // 2, LANES), (0, 2, 3, 1))
        o_a = _fox_attention(sh(q), kt, sh(v), fr)
        qq, qidx, kk = _dsa_prep(cq, small3, tp, q_norm_g[l], kv_norm_g[l], kidx_ln_g[l], kidx_ln_b[l],
                                 _layout_w_q(w_uq[l], w_qidx[l]), _uk_per_head(w_uk[l]).astype(BF16))
        o_b = _dsa_attention(sh(qq), sh(qidx), sh(small3), sh(kk), _pair_blockdiag(w_uv[l]).astype(BF16), t, k_sel)
        w_r = jnp.pad(w_router[l], ((0, 0), (0, LANES - N_EXPERTS))).astype(BF16)
        b_r = jnp.pad(b_router[l].astype(F32), (0, LANES - N_EXPERTS), constant_values=NEG_INF)[None]
        h1, hn, ti, tg = _merge_router(
            o_a, o_b, sh(sga), sh(sgb), hp, t, w_branch_a[l].astype(BF16), w_branch_b[l].astype(BF16),
            w_out[l].astype(BF16), norm_ffn_g[l].astype(F32)[None], w_r, b_r)
        n_tok = batch * t
        block_expert, slot_tok, slot_dst = _routing_tables(ti[:, :, :TOP_K], n_tok)
        y = _experts(hn.reshape(n_tok * (d // LANES), LANES), block_expert, slot_tok, slot_dst,
                     w_gate_up[l].astype(BF16), b_gate_up[l].astype(F32)[:, None, :],
                     w_down[l].astype(BF16), b_down[l].astype(F32)[:, None, :])
        h = _combine(h1, y, tg, norm_final_g.astype(F32)[None], final_norm=(l == depth - 1))
    return h[:, N_META:]
```

```python
import functools

import numpy as np
import jax
import jax.numpy as jnp
from jax import lax
from jax.experimental import pallas as pl
from jax.experimental.pallas import tpu as pltpu

F32 = jnp.float32
BF16 = jnp.bfloat16
I32 = jnp.int32

D_MODEL = 1024
CHUNK = 64
N_META = 16
Q_BLOCK = 128
ROPE_THETA = 10000.0
NORM_EPS = 1e-6
NEG_INF = -1e30

FOX_HEADS = 8
FOX_HEAD_DIM = 64
DSA_HEADS = 8
Q_RANK = 256
KV_RANK = 128
QK_NOPE_DIM = 64
QK_ROPE_DIM = 32
V_HEAD_DIM = 64
IDX_HEADS = 4
IDX_HEAD_DIM = 64
IDX_ROPE_DIM = 32
IDX_TOPK_MAX = 256
N_EXPERTS = 32
TOP_K = 4
D_EXPERT = D_MODEL
SWIGLU_ALPHA = 1.702
SWIGLU_LIMIT = 7.0

FOX_WIDTH = FOX_HEADS * FOX_HEAD_DIM
DSA_WIDTH = DSA_HEADS * V_HEAD_DIM
IN_SPLITS = (FOX_WIDTH, FOX_WIDTH, FOX_WIDTH, FOX_HEADS, Q_RANK, KV_RANK, QK_ROPE_DIM,
             IDX_HEAD_DIM, IDX_HEADS, D_MODEL, D_MODEL)

LANES = 128
VMEM_LIMIT = 56 * 1024 * 1024
KEY_GROUP = 512
FOX_KEY_GROUP = 256
KEY_TILE = 256
Q_ROWS = 256
LOG2E = 1.4426950408889634
EXPERT_ROWS = 256
EXPERT_CHUNKS = 4

KR_LO, KIDX_LO = 0, 32
FA_LO, IW_LO = 64, 72

C_Q, C_K, C_V = 0, 512, 1024
C_CQ = 1536
C_G1, C_G2, C_G3 = 1792, 1920, 2048
C_GA, C_GB = 2176, 3200
C_TOTAL = 4224


def _cparams(sem):
    return pltpu.CompilerParams(dimension_semantics=sem, vmem_limit_bytes=VMEM_LIMIT)


def _row_tile(rows, cap):
    t = cap
    while rows % t:
        t //= 2
    return t


def _rot_cols(w):
    h = w.shape[-1] // 2
    return jnp.concatenate([-w[..., h:], w[..., :h]], axis=-1)


def _perm_cols(w):
    h = w.shape[-1] // 2
    return jnp.concatenate([w[..., h:], w[..., :h]], axis=-1)


def _layout_w_in(w_in):
    offs = np.cumsum((0,) + IN_SPLITS)
    qa, ka, va, fa, cq, ckv, kr, kidx, iw, ga, gb = [w_in[:, offs[i]:offs[i + 1]] for i in range(11)]
    d = w_in.shape[0]
    z = lambda n: jnp.zeros((d, n), w_in.dtype)
    g2 = jnp.concatenate([kr, kidx, z(32)], axis=1)
    g3 = jnp.concatenate([_rot_cols(kr), _perm_cols(kidx[:, :IDX_ROPE_DIM]), fa, iw, z(52)], axis=1)
    w = jnp.concatenate([qa, ka, va, cq, ckv, g2, g3, ga, gb], axis=1)
    assert w.shape[1] == C_TOTAL
    return w.astype(BF16)


def _layout_w_q(w_uq, w_qidx):
    r = w_uq.shape[0]
    wq = w_uq.reshape(r, DSA_HEADS, QK_NOPE_DIM + QK_ROPE_DIM)
    nope = wq[:, :, :QK_NOPE_DIM].reshape(r, DSA_HEADS * QK_NOPE_DIM)
    rope = wq[:, :, QK_NOPE_DIM:]
    pad_r = lambda a: jnp.pad(a, ((0, 0), (0, 0), (KR_LO, LANES - KR_LO - QK_ROPE_DIM))).reshape(r, -1)
    wi = w_qidx.reshape(r, IDX_HEADS, IDX_HEAD_DIM)
    wi_rot = jnp.concatenate([_rot_cols(wi[:, :, :IDX_ROPE_DIM]), jnp.zeros_like(wi[:, :, IDX_ROPE_DIM:])], axis=-1)
    pad_i = lambda a: jnp.pad(a, ((0, 0), (0, 0), (KIDX_LO, LANES - KIDX_LO - IDX_HEAD_DIM))).reshape(r, -1)
    w = jnp.concatenate([nope, pad_r(rope), pad_r(_rot_cols(rope)), pad_i(wi), pad_i(wi_rot)], axis=1)
    return w.astype(BF16)


def _rope_tables(tp):
    half = QK_ROPE_DIM // 2
    inv = ROPE_THETA ** (-jnp.arange(half, dtype=F32) / half)
    ang = jnp.arange(tp, dtype=F32)[:, None] * inv[None, :]
    cos, sin = jnp.cos(ang), jnp.sin(ang)
    cos32 = jnp.concatenate([cos, cos], axis=1)
    sin32 = jnp.concatenate([sin, sin], axis=1)
    one, zero = jnp.ones((tp, 1), F32), jnp.zeros((tp, 1), F32)
    cos_k = jnp.concatenate([cos32, cos32, jnp.tile(one, (1, 64))], axis=1)
    sin_k = jnp.concatenate([sin32, sin32, jnp.tile(zero, (1, 64))], axis=1)
    return cos_k, sin_k


def _inproj_kernel(x_ref, g_ref, w_ref, q_ref, k_ref, v_ref, cq_ref, sm_ref, ga_ref, gb_ref):
    x = x_ref[...]
    ms = jnp.mean(x * x, axis=-1, keepdims=True)
    xn = (x * lax.rsqrt(ms + NORM_EPS) * g_ref[...]).astype(BF16)

    def mm(lo, hi):
        return jnp.dot(xn, w_ref[:, lo:hi], preferred_element_type=F32)

    q_ref[...] = (mm(C_Q, C_K) * (FOX_HEAD_DIM ** -0.5 * LOG2E)).astype(BF16)
    k_ref[...] = mm(C_K, C_V).astype(BF16)
    v_ref[...] = mm(C_V, C_CQ).astype(BF16)
    cq_ref[...] = mm(C_CQ, C_G1)
    sm_ref[...] = mm(C_G1, C_GA)
    ga_ref[...] = (1.0 / (1.0 + jnp.exp(-mm(C_GA, C_GB)))).astype(BF16)
    gb_ref[...] = (1.0 / (1.0 + jnp.exp(-mm(C_GB, C_TOTAL)))).astype(BF16)


def _in_proj(hp, g, w_cat):
    rows, d = hp.shape
    tm = _row_tile(rows, 512)
    row = lambda n: pl.BlockSpec((tm, n), lambda i: (i, 0))
    full = lambda a: pl.BlockSpec(a.shape, lambda i: (0,) * a.ndim)
    outs = [(FOX_WIDTH, BF16)] * 3 + [(Q_RANK, F32), (3 * LANES, F32), (D_MODEL, BF16), (D_MODEL, BF16)]
    return pl.pallas_call(
        _inproj_kernel,
        grid=(rows // tm,),
        in_specs=[row(d), full(g), full(w_cat)],
        out_specs=[row(n) for n, _ in outs],
        out_shape=[jax.ShapeDtypeStruct((rows, n), dt) for n, dt in outs],
        compiler_params=_cparams(("parallel",)),
        name="in_proj",
    )(hp, g, w_cat)


def _split3(x):
    hi = x.astype(BF16)
    r1 = x - hi.astype(F32)
    mid = r1.astype(BF16)
    lo = (r1 - mid.astype(F32)).astype(BF16)
    return hi, mid, lo


def _fcum_kernel(sm_ref, b_ref, tri_ref, o_ref):
    tp = o_ref.shape[1]
    tri = tri_ref[...]
    carry = jnp.zeros((1, LANES), F32)
    for j in range(tp // LANES):
        z = sm_ref[0, j * LANES:(j + 1) * LANES, :] + b_ref[...]
        ls = jnp.minimum(z, 0.0) - jnp.log(1.0 + jnp.exp(-jnp.abs(z)))
        c = carry
        for part in _split3(ls):
            c = c + jnp.dot(tri, part, preferred_element_type=F32)
        o_ref[0, j * LANES:(j + 1) * LANES, :] = c * LOG2E
        carry = c[LANES - 1:LANES, :]


def _forget_cumsum(small3, b_forget, batch, tp):
    g3 = small3.reshape(batch, tp, 3 * LANES)
    bias = jnp.zeros((1, LANES), F32).at[0, FA_LO:FA_LO + FOX_HEADS].set(b_forget.astype(F32))
    tri = jnp.tril(jnp.ones((LANES, LANES), F32)).astype(BF16)
    return pl.pallas_call(
        _fcum_kernel,
        grid=(batch,),
        in_specs=[pl.BlockSpec((1, tp, LANES), lambda b: (b, 0, 2)),
                  pl.BlockSpec((1, LANES), lambda b: (0, 0)),
                  pl.BlockSpec((LANES, LANES), lambda b: (0, 0))],
        out_specs=pl.BlockSpec((1, tp, LANES), lambda b: (b, 0, 0)),
        out_shape=jax.ShapeDtypeStruct((batch, tp, LANES), F32),
        compiler_params=_cparams(("parallel",)),
        name="fcum",
    )(g3, bias, tri)


def _key_widths(tp, group):
    ws = list(range(group, tp, group)) + [tp]
    return ws


def _query_steps(tp):
    return -(-tp // Q_ROWS)


def _for_key_width(qi, tp, body, q_rows=Q_ROWS, lookahead=0, group=KEY_GROUP):
    groups = []
    for step in range(-(-tp // q_rows)):
        rows = min(q_rows, tp - step * q_rows)
        need = min(tp, step * q_rows + rows + lookahead)
        w = min(x for x in _key_widths(tp, group) if x >= need)
        if groups and groups[-1][:2] == [rows, w]:
            groups[-1][3] = step
        else:
            groups.append([rows, w, step, step])
    for rows, w, first, last in groups:
        pl.when((qi >= first) & (qi <= last))(functools.partial(body, rows, w, first))


def _key_tiles(w):
    return [(k0, min(KEY_TILE, w - k0)) for k0 in range(0, w, KEY_TILE)]


def _lane_group_max(mx, s):
    for c in range(s.shape[1] // LANES):
        mx = jnp.maximum(mx, s[:, c * LANES:(c + 1) * LANES])
    return mx


def _fox_kernel(q_ref, kt_ref, v_ref, fr_ref, o_ref, s_ref):
    qi = pl.program_id(2)
    tp = kt_ref.shape[3]

    def body(rows, w, first):
        q = q_ref[0, :rows]
        lane = lax.broadcasted_iota(I32, (rows, LANES), 1)
        zero = jnp.zeros_like(q)
        qs = jnp.concatenate([jnp.where(lane < FOX_HEAD_DIM, q, zero),
                              jnp.where(lane >= FOX_HEAD_DIM, q, zero)], axis=0)
        mxs = [jnp.full((rows, LANES), -jnp.inf, F32)] * 2
        for k0, tk in _key_tiles(w):
            s2 = jnp.dot(qs, kt_ref[0, 0, :, k0:k0 + tk], preferred_element_type=F32)
            for hh in range(2):
                s = s2[hh * rows:(hh + 1) * rows] - fr_ref[0, 0, hh:hh + 1, k0:k0 + tk]
                if k0 + tk > first * Q_ROWS:
                    qpos = qi * Q_ROWS + lax.broadcasted_iota(I32, (rows, tk), 0)
                    kpos = k0 + lax.broadcasted_iota(I32, (rows, tk), 1)
                    s = jnp.where(kpos <= qpos, s, NEG_INF)
                s_ref[hh, :rows, k0:k0 + tk] = s
                mxs[hh] = _lane_group_max(mxs[hh], s)
        ms = [jnp.max(mx, axis=-1, keepdims=True) for mx in mxs]
        acc = jnp.zeros((2 * rows, 2 * LANES), F32)
        for k0, tk in _key_tiles(w):
            p2 = jnp.concatenate([jnp.exp2((s_ref[hh, :rows, k0:k0 + tk] - ms[hh]).astype(BF16))
                                  for hh in range(2)], axis=0)
            ones_lane = jnp.where(lax.broadcasted_iota(I32, (tk, LANES), 1) == 0, 1.0, 0.0).astype(BF16)
            v_aug = jnp.concatenate([v_ref[0, k0:k0 + tk, :], ones_lane], axis=1)
            acc = acc + jnp.dot(p2, v_aug, preferred_element_type=F32)
        o = acc[:, :LANES] * (1.0 / acc[:, LANES:LANES + 1])
        o_ref[0, :rows] = jnp.where(lane < FOX_HEAD_DIM, o[:rows], o[rows:]).astype(BF16)

    _for_key_width(qi, tp, body, group=FOX_KEY_GROUP)


def _fox_attention(q, kt, v, fr):
    batch, tp, _ = q.shape
    nq = _query_steps(tp)
    pairs = FOX_HEADS // 2
    return pl.pallas_call(
        _fox_kernel,
        grid=(batch, pairs, nq),
        in_specs=[pl.BlockSpec((1, Q_ROWS, LANES), lambda b, p, i: (b, i, p)),
                  pl.BlockSpec((1, 1, LANES, tp), lambda b, p, i: (b, p, 0, 0)),
                  pl.BlockSpec((1, tp, LANES), lambda b, p, i: (b, 0, p)),
                  pl.BlockSpec((1, 1, 2, tp), lambda b, p, i: (b, p, 0, 0))],
        out_specs=pl.BlockSpec((1, Q_ROWS, LANES), lambda b, p, i: (b, i, p)),
        out_shape=jax.ShapeDtypeStruct((batch, tp, FOX_WIDTH), BF16),
        scratch_shapes=[pltpu.VMEM((2, Q_ROWS, tp), F32)],
        compiler_params=_cparams(("parallel", "parallel", "arbitrary")),
        name="fox",
    )(q, kt, v, fr)


def _dsa_prep_kernel(cq_ref, sm_ref, qg_ref, kvg_ref, lng_ref, lnb_ref, lngp_ref, lnbp_ref,
                     cosk_ref, sink_ref, wq_ref, wuk_ref, qq_ref, qidx_ref, kk_ref):
    cq = cq_ref[...]
    ms = jnp.mean(cq * cq, axis=-1, keepdims=True)
    cqn = (cq * lax.rsqrt(ms + NORM_EPS) * qg_ref[...]).astype(BF16)
    cos = cosk_ref[...]
    sin = sink_ref[...]

    def mm(lo, hi):
        return jnp.dot(cqn, wq_ref[:, lo:hi], preferred_element_type=F32)

    nope_w = DSA_HEADS * QK_NOPE_DIM
    hw = DSA_HEADS * LANES
    iw = IDX_HEADS * LANES
    q_nope = mm(0, nope_w).astype(BF16)
    for h in range(DSA_HEADS):
        base = nope_w + h * LANES
        q_rope = mm(base, base + LANES) * cos + mm(base + hw, base + hw + LANES) * sin
        p = h // 2
        q_lat = jnp.dot(q_nope[:, p * LANES:(p + 1) * LANES], wuk_ref[h], preferred_element_type=F32)
        qscale = (QK_NOPE_DIM + QK_ROPE_DIM) ** -0.5 * LOG2E
        qq_ref[:, h * 2 * LANES:h * 2 * LANES + LANES] = (q_lat * qscale).astype(BF16)
        qq_ref[:, h * 2 * LANES + LANES:(h + 1) * 2 * LANES] = (q_rope * qscale).astype(BF16)
    for h in range(IDX_HEADS):
        base = nope_w + 2 * hw + h * LANES
        qi = mm(base, base + LANES) * cos + mm(base + iw, base + iw + LANES) * sin
        qidx_ref[:, h * LANES:(h + 1) * LANES] = qi.astype(BF16)

    ckv = sm_ref[:, 0:LANES]
    ms = jnp.mean(ckv * ckv, axis=-1, keepdims=True)
    kk_ref[:, 0:LANES] = (ckv * lax.rsqrt(ms + NORM_EPS) * kvg_ref[...]).astype(BF16)
    g2 = sm_ref[:, LANES:2 * LANES]
    g3 = sm_ref[:, 2 * LANES:3 * LANES]
    lane = lax.broadcasted_iota(I32, g2.shape, 1)
    is_idx = (lane >= KIDX_LO) & (lane < KIDX_LO + IDX_HEAD_DIM)
    is_rope = lane < KIDX_LO
    xi = jnp.where(is_idx, g2, 0.0)
    mu = jnp.sum(xi, axis=-1, keepdims=True) * (1.0 / IDX_HEAD_DIM)
    xc = jnp.where(is_idx, g2 - mu, 0.0)
    var = jnp.sum(xc * xc, axis=-1, keepdims=True) * (1.0 / IDX_HEAD_DIM)
    rstd = lax.rsqrt(var + NORM_EPS)
    y = xc * rstd * lng_ref[...] + lnb_ref[...]
    yp = (g3 - mu) * rstd * lngp_ref[...] + lnbp_ref[...]
    a = jnp.where(is_rope, g2, y)
    b = jnp.where(is_rope, g3, jnp.where(lane < KIDX_LO + IDX_ROPE_DIM, yp, 0.0))
    kx = jnp.where(lane == LANES - 1, 1.0, a * cos + b * sin)
    kk_ref[:, LANES:2 * LANES] = kx.astype(BF16)


def _dsa_prep(cq, small3, tp, q_norm_g, kv_norm_g, ln_g, ln_b, w_q, w_uk_pair):
    rows = cq.shape[0]
    tm = tp // 4 if tp % 64 == 0 else tp
    per = tp // tm
    cos_k, sin_k = _rope_tables(tp)
    lane_vec = lambda v, lo: jnp.zeros((1, LANES), F32).at[0, lo:lo + v.shape[0]].set(v.astype(F32))
    half = IDX_ROPE_DIM // 2
    sign = jnp.concatenate([-jnp.ones((half,), F32), jnp.ones((half,), F32)])
    g_perm = jnp.concatenate([ln_g[half:IDX_ROPE_DIM], ln_g[:half]]).astype(F32) * sign
    b_perm = jnp.concatenate([ln_b[half:IDX_ROPE_DIM], ln_b[:half]]).astype(F32) * sign
    row = lambda n: pl.BlockSpec((tm, n), lambda i: (i, 0))
    tab = pl.BlockSpec((tm, LANES), lambda i: (i % per, 0))
    full = lambda a: pl.BlockSpec(a.shape, lambda i: (0,) * a.ndim)
    vecs = [q_norm_g.astype(F32)[None], kv_norm_g.astype(F32)[None], lane_vec(ln_g, KIDX_LO),
            lane_vec(ln_b, KIDX_LO), lane_vec(g_perm, KIDX_LO), lane_vec(b_perm, KIDX_LO)]
    outs = [(DSA_HEADS * 2 * LANES, BF16), (IDX_HEADS * LANES, BF16), (2 * LANES, BF16)]
    return pl.pallas_call(
        _dsa_prep_kernel,
        grid=(rows // tm,),
        in_specs=[row(Q_RANK), row(3 * LANES)] + [full(v) for v in vecs] + [tab, tab, full(w_q), full(w_uk_pair)],
        out_specs=[row(n) for n, _ in outs],
        out_shape=[jax.ShapeDtypeStruct((rows, n), dt) for n, dt in outs],
        compiler_params=_cparams(("parallel",)),
        name="dsa_prep",
    )(cq, small3, *vecs, cos_k, sin_k, w_q, w_uk_pair)


def _dsa_kernel(qq_ref, qidx_ref, sm_ref, kk_ref, tri_ref, wuv_ref, o_ref, key_ref, bias_ref, *,
                n_valid, k_sel):
    qi = pl.program_id(1)
    tp = kk_ref.shape[1]
    int_min = jnp.int32(-2 ** 31)

    def body(rows, w, first):
        del first
        kx = kk_ref[0, :w, LANES:]
        acc = jnp.zeros((rows, w), F32)
        for h in range(IDX_HEADS):
            d = lax.dot_general(qidx_ref[0, :, h * LANES:(h + 1) * LANES], kx,
                                (((1,), (1,)), ((), ())), preferred_element_type=F32)
            wh = sm_ref[0, :, IW_LO + h:IW_LO + h + 1] * (IDX_HEADS ** -0.5)
            acc = acc + wh * jnp.maximum(d, 0.0)
        isc = acc * (IDX_HEAD_DIM ** -0.5)
        isc = jnp.where(isc == 0.0, 0.0, isc)
        qpos = qi * rows + lax.broadcasted_iota(I32, (rows, 1), 0)
        kend = (((qpos + (CHUNK - N_META)) // CHUNK) + 1) * CHUNK - (CHUNK - N_META)
        kend = jnp.minimum(kend, n_valid)
        kpos = lax.broadcasted_iota(I32, (rows, w), 1)
        adm = kpos < kend
        bits = pltpu.bitcast(isc, I32)
        key = bits ^ ((bits >> 31) & jnp.int32(0x7FFFFFFF))
        key = jnp.where(adm, key, int_min)
        key_ref[:, :w] = key
        ok_bias = jnp.where(adm & (isc > 0.5 * NEG_INF), 0.0, NEG_INF)

        def count_ge(c):
            return jnp.sum(jnp.where(key_ref[:, :w] >= c, 1.0, 0.0), axis=-1, keepdims=True)

        kf = jnp.float32(k_sel)
        t0 = jnp.where(count_ge(jnp.zeros((rows, 1), I32)) >= kf, jnp.int32(0), int_min)

        def bit_step(i, t):
            cand = t | jnp.left_shift(jnp.int32(1), 30 - i)
            return jnp.where(count_ge(cand) >= kf, cand, t)

        thr = lax.fori_loop(0, 31, bit_step, t0)

        gt = key > thr
        need = kf - jnp.sum(jnp.where(gt, 1.0, 0.0), axis=-1, keepdims=True)
        run = jnp.zeros((rows, 1), F32)
        tri = tri_ref[...]
        for j in range(w // LANES):
            sl = slice(j * LANES, (j + 1) * LANES)
            eq = key[:, sl] == thr
            e = jnp.where(eq, 1.0, 0.0)
            pre = jnp.dot(e.astype(BF16), tri, preferred_element_type=F32) + run
            take = gt[:, sl] | (eq & (pre <= need))
            bias_ref[:, sl] = jnp.where(take, ok_bias[:, sl], NEG_INF)
            run = pre[:, LANES - 1:LANES]

        bias = bias_ref[:, :w]
        kk = kk_ref[0, :w, :]
        kv = kk_ref[0, :w, :LANES]
        for p in range(DSA_HEADS // 2):
            lat = []
            for h in (2 * p, 2 * p + 1):
                qh = qq_ref[0, :, h * 2 * LANES:(h + 1) * 2 * LANES]
                s = lax.dot_general(qh, kk, (((1,), (1,)), ((), ())), preferred_element_type=F32) + bias
                m = jnp.max(s, axis=-1, keepdims=True)
                pr = jnp.exp2(s - m)
                l = jnp.sum(pr, axis=-1, keepdims=True)
                lat.append((jnp.dot(pr.astype(BF16), kv, preferred_element_type=F32) * (1.0 / l)).astype(BF16))
            o_lat = jnp.concatenate(lat, axis=1)
            o_ref[0, :, p * LANES:(p + 1) * LANES] = jnp.dot(
                o_lat, wuv_ref[p], preferred_element_type=F32).astype(BF16)

    _for_key_width(qi, tp, body, q_rows=Q_BLOCK, lookahead=Q_BLOCK)


def _dsa_attention(qq, qidx, small3, kk, w_uv_pair, n_valid, k_sel):
    batch, tp, _ = kk.shape
    nq = tp // Q_BLOCK
    tri = jnp.triu(jnp.ones((LANES, LANES), F32)).astype(BF16)
    qblk = lambda n: pl.BlockSpec((1, Q_BLOCK, n), lambda b, i: (b, i, 0))
    return pl.pallas_call(
        functools.partial(_dsa_kernel, n_valid=n_valid, k_sel=k_sel),
        grid=(batch, nq),
        in_specs=[qblk(DSA_HEADS * 2 * LANES), qblk(IDX_HEADS * LANES),
                  pl.BlockSpec((1, Q_BLOCK, LANES), lambda b, i: (b, i, 2)),
                  pl.BlockSpec((1, tp, 2 * LANES), lambda b, i: (b, 0, 0)),
                  pl.BlockSpec((LANES, LANES), lambda b, i: (0, 0)),
                  pl.BlockSpec(w_uv_pair.shape, lambda b, i: (0, 0, 0))],
        out_specs=qblk(DSA_WIDTH),
        out_shape=jax.ShapeDtypeStruct((batch, tp, DSA_WIDTH), BF16),
        scratch_shapes=[pltpu.VMEM((Q_BLOCK, tp), I32), pltpu.VMEM((Q_BLOCK, tp), F32)],
        compiler_params=_cparams(("parallel", "arbitrary")),
        name="dsa",
    )(qq, qidx, small3, kk, tri, w_uv_pair)


def _store_row_tiles(ref, x):
    rows, d = x.shape
    sub = d // LANES
    for j in range(sub):
        ref[pl.ds(j, rows, stride=sub), :] = x[:, j * LANES:(j + 1) * LANES]


def _load_row_tiles(ref, rows):
    sub = ref.shape[0] // rows
    return jnp.concatenate([ref[pl.ds(j, rows, stride=sub), :] for j in range(sub)], axis=1)


def _merge_kernel(oa_ref, ob_ref, ga_ref, gb_ref, h_ref, wa_ref, wb_ref, wo_ref, g_ref, wr_ref, br_ref,
                  h1_ref, hn_ref, ti_ref, tg_ref):
    a = jnp.dot(oa_ref[0], wa_ref[...], preferred_element_type=F32)
    b = jnp.dot(ob_ref[0], wb_ref[...], preferred_element_type=F32)
    merged = ga_ref[0].astype(F32) * a + gb_ref[0].astype(F32) * b
    mix = jnp.dot(merged.astype(BF16), wo_ref[...], preferred_element_type=F32)
    h1 = h_ref[0] + mix
    h1_ref[0] = h1
    ms = jnp.mean(h1 * h1, axis=-1, keepdims=True)
    hn = h1 * lax.rsqrt(ms + NORM_EPS) * g_ref[...]
    _store_row_tiles(hn_ref.at[0], hn)
    logits = jnp.dot(hn.astype(BF16), wr_ref[...], preferred_element_type=F32) + br_ref[...]
    lane = lax.broadcasted_iota(I32, logits.shape, 1)
    idx_out = jnp.zeros(logits.shape, I32)
    val_out = jnp.zeros(logits.shape, F32)
    vals = []
    for kk in range(TOP_K):
        m = jnp.max(logits, axis=-1, keepdims=True)
        idx = jnp.min(jnp.where(logits == m, lane, LANES), axis=-1, keepdims=True)
        logits = jnp.where(lane == idx, -jnp.inf, logits)
        idx_out = jnp.where(lane == kk, idx, idx_out)
        vals.append(m)
    es = [jnp.exp(v - vals[0]) for v in vals]
    den = es[0] + es[1] + es[2] + es[3]
    for kk in range(TOP_K):
        val_out = jnp.where(lane == kk, es[kk] / den, val_out)
    ti_ref[0] = idx_out
    tg_ref[0] = val_out


def _merge_router(o_a, o_b, sga, sgb, hp, t_valid, w_a, w_b, w_o, g_ffn, w_r, b_r):
    batch, tp, d = hp.shape
    tt = t_valid // 3 if t_valid % 48 == 0 else t_valid
    blk = lambda n: pl.BlockSpec((1, tt, n), lambda b, j: (b, j, 0))
    full = lambda a: pl.BlockSpec(a.shape, lambda b, j: (0,) * a.ndim)
    consts = [w_a, w_b, w_o, g_ffn, w_r, b_r]
    sub = d // LANES
    return pl.pallas_call(
        _merge_kernel,
        grid=(batch, t_valid // tt),
        in_specs=[blk(FOX_WIDTH), blk(DSA_WIDTH), blk(d), blk(d), blk(d)] + [full(c) for c in consts],
        out_specs=[blk(d), pl.BlockSpec((1, tt * sub, LANES), lambda b, j: (b, j, 0)), blk(LANES), blk(LANES)],
        out_shape=[jax.ShapeDtypeStruct((batch, t_valid, d), F32),
                   jax.ShapeDtypeStruct((batch, t_valid * sub, LANES), F32),
                   jax.ShapeDtypeStruct((batch, t_valid, LANES), I32),
                   jax.ShapeDtypeStruct((batch, t_valid, LANES), F32)],
        compiler_params=_cparams(("parallel", "parallel")),
        name="merge_router",
    )(o_a, o_b, sga, sgb, hp, *consts)


def _experts_kernel(be_ref, tok_ref, tokn_ref, dstp_ref, dst_ref, hn_hbm, wgu_ref, bgu_ref, wd_ref, bd_ref,
                    y_hbm, x0, x1, y0, y1, xb_ref, act_ref, gsem, ssem):
    i = pl.program_id(0)
    nb = pl.num_programs(0)
    sub = wd_ref.shape[2] // LANES
    rows = x0.shape[0] // sub
    xbufs, ybufs = (x0, x1), (y0, y1)

    def gather_row(idx_ref, s, r):
        src = pl.multiple_of(idx_ref[0, 0, r], sub)
        return pltpu.make_async_copy(hn_hbm.at[pl.ds(src, sub)], xbufs[s].at[pl.ds(r * sub, sub)], gsem.at[s])

    def scatter_row(idx_ref, s, r):
        dst = pl.multiple_of(idx_ref[0, 0, r], sub)
        return pltpu.make_async_copy(ybufs[s].at[pl.ds(r * sub, sub)], y_hbm.at[pl.ds(dst, sub)], ssem.at[s])

    def wait_gather(s):
        pltpu.make_async_copy(hn_hbm.at[pl.ds(0, rows * sub)], xbufs[s], gsem.at[s]).wait()

    def wait_scatter(s):
        pltpu.make_async_copy(ybufs[s], y_hbm.at[pl.ds(0, rows * sub)], ssem.at[s]).wait()

    @pl.when(i == 0)
    def _():
        y1[...] = jnp.zeros(y1.shape, y1.dtype)

        def issue(r, c):
            gather_row(tok_ref, 0, r).start()
            return c
        lax.fori_loop(0, rows, issue, 0)

    def step(slot):
        other = 1 - slot

        @pl.when(i >= 1)
        def _():
            wait_scatter(slot)

        wait_gather(slot)
        xb_ref[...] = _load_row_tiles(xbufs[slot], rows).astype(BF16)
        per = rows // EXPERT_CHUNKS
        cw = D_EXPERT // EXPERT_CHUNKS
        for c in range(EXPERT_CHUNKS):
            @pl.when(be_ref[i] >= 0)
            def _():
                for r in range(c * per, (c + 1) * per):
                    gather_row(tokn_ref, other, r).start()
                for r in range(c * per, (c + 1) * per):
                    scatter_row(dstp_ref, other, r).start()
                x = xb_ref[...]
                glu = jnp.dot(x, wgu_ref[0, :, c * cw:(c + 1) * cw], preferred_element_type=F32)
                glu = jnp.minimum(glu + bgu_ref[0, :, c * cw:(c + 1) * cw], SWIGLU_LIMIT)
                lo = D_EXPERT + c * cw
                lin = jnp.dot(x, wgu_ref[0, :, lo:lo + cw], preferred_element_type=F32)
                lin = jnp.clip(lin + bgu_ref[0, :, lo:lo + cw], -SWIGLU_LIMIT, SWIGLU_LIMIT)
                act = glu * (1.0 / (1.0 + jnp.exp(-SWIGLU_ALPHA * glu))) * (lin + 1.0)
                act_ref[:, c * cw:(c + 1) * cw] = act.astype(BF16)

        _store_row_tiles(ybufs[slot],
                         jnp.dot(act_ref[...], wd_ref[0], preferred_element_type=F32) + bd_ref[0])

        @pl.when(i == nb - 1)
        def _():
            def issue(r, c):
                scatter_row(dst_ref, slot, r).start()
                return c
            lax.fori_loop(0, rows, issue, 0)
            wait_scatter(slot)
            wait_scatter(other)
            wait_gather(other)

    pl.when(i % 2 == 0)(functools.partial(step, 0))
    pl.when(i % 2 == 1)(functools.partial(step, 1))


def _experts(hn, block_expert, slot_tok, slot_dst, w_gu, b_gu, w_d, b_d):
    d = w_d.shape[2]
    sub = d // LANES
    n_slots = slot_tok.shape[0]
    rows = EXPERT_ROWS
    nb = n_slots // rows
    tok3 = (slot_tok * sub).reshape(nb, 1, rows)
    dst3 = (slot_dst * sub).reshape(nb, 1, rows)
    smem = lambda f: pl.BlockSpec((1, 1, rows), f, memory_space=pltpu.SMEM)
    grid_spec = pltpu.PrefetchScalarGridSpec(
        num_scalar_prefetch=1,
        grid=(nb,),
        in_specs=[smem(lambda i, be: (i, 0, 0)),
                  smem(lambda i, be: (jnp.minimum(i + 1, nb - 1), 0, 0)),
                  smem(lambda i, be: (jnp.maximum(i - 1, 0), 0, 0)),
                  smem(lambda i, be: (i, 0, 0)),
                  pl.BlockSpec(memory_space=pl.ANY),
                  pl.BlockSpec((1, d, 2 * D_EXPERT), lambda i, be: (be[i], 0, 0)),
                  pl.BlockSpec((1, 1, 2 * D_EXPERT), lambda i, be: (be[i], 0, 0)),
                  pl.BlockSpec((1, D_EXPERT, d), lambda i, be: (be[i], 0, 0)),
                  pl.BlockSpec((1, 1, d), lambda i, be: (be[i], 0, 0))],
        out_specs=pl.BlockSpec(memory_space=pl.ANY),
        scratch_shapes=[pltpu.VMEM((rows * sub, LANES), F32)] * 4 +
                       [pltpu.VMEM((rows, d), BF16), pltpu.VMEM((rows, D_EXPERT), BF16)] +
                       [pltpu.SemaphoreType.DMA((2,)), pltpu.SemaphoreType.DMA((2,))],
    )
    return pl.pallas_call(
        _experts_kernel,
        grid_spec=grid_spec,
        out_shape=jax.ShapeDtypeStruct((n_slots * sub, LANES), F32),
        compiler_params=_cparams(("arbitrary",)),
        name="experts",
    )(block_expert, tok3, tok3, dst3, dst3, hn, w_gu, b_gu, w_d, b_d)


def _routing_tables(top_idx, n_tok):
    rows = EXPERT_ROWS
    n_asg = n_tok * TOP_K
    flat_e = top_idx.reshape(-1).astype(I32)
    keys = jnp.sort(flat_e * n_asg + jnp.arange(n_asg, dtype=I32))
    order = keys % n_asg
    counts = jnp.sum(flat_e[:, None] == jnp.arange(N_EXPERTS, dtype=I32)[None, :], axis=0).astype(I32)
    padded = (counts + rows - 1) // rows * rows
    start = jnp.cumsum(counts) - counts
    pend = jnp.cumsum(padded)
    pstart = pend - padded
    nb = -(-n_asg // rows) + N_EXPERTS
    n_slots = nb * rows
    block_start = jnp.arange(nb, dtype=I32) * rows
    block_expert = jnp.minimum(
        jnp.sum((pend[None, :] <= block_start[:, None]).astype(I32), axis=1), N_EXPERTS - 1).astype(I32)
    slot = jnp.arange(n_slots, dtype=I32)
    e = jnp.repeat(block_expert, rows)
    r = slot - pstart[e]
    valid = (r >= 0) & (r < counts[e])
    asg = order[jnp.clip(start[e] + r, 0, n_asg - 1)]
    slot_tok = jnp.where(valid, asg // TOP_K, 0).astype(I32)
    pad_rank = jnp.cumsum(jnp.logical_not(valid).astype(I32)) - 1
    slot_dst = jnp.where(valid, (asg % TOP_K) * n_tok + asg // TOP_K, n_asg + pad_rank).astype(I32)
    return block_expert, slot_tok, slot_dst


def _combine_kernel(h_ref, y0_ref, y1_ref, y2_ref, y3_ref, tg_ref, g_ref, o_ref, *, final_norm):
    h = h_ref[0]
    for kk, y_ref in enumerate((y0_ref, y1_ref, y2_ref, y3_ref)):
        h = h + _load_row_tiles(y_ref, h.shape[0]) * tg_ref[0, :, kk:kk + 1]
    if final_norm:
        ms = jnp.mean(h * h, axis=-1, keepdims=True)
        h = h * lax.rsqrt(ms + NORM_EPS) * g_ref[...]
    o_ref[0] = h


def _combine(h1, y, tg, g, final_norm):
    batch, t, d = h1.shape
    tt = t // 6 if t % 48 == 0 else t
    per_batch = t // tt
    per_pick = batch * per_batch
    blk = lambda n: pl.BlockSpec((1, tt, n), lambda b, j: (b, j, 0))
    pick = lambda kk: pl.BlockSpec((tt * (d // LANES), LANES), lambda b, j: (kk * per_pick + b * per_batch + j, 0))
    return pl.pallas_call(
        functools.partial(_combine_kernel, final_norm=final_norm),
        grid=(batch, per_batch),
        in_specs=[blk(d)] + [pick(kk) for kk in range(TOP_K)] + [blk(LANES), pl.BlockSpec((1, d), lambda b, j: (0, 0))],
        out_specs=blk(d),
        out_shape=jax.ShapeDtypeStruct((batch, t, d), F32),
        compiler_params=_cparams(("parallel", "parallel")),
        name="combine",
    )(h1, y, y, y, y, tg, g)


def _pair_blockdiag(w):
    h, a, b = w.shape
    w = w.reshape(h // 2, 2, a, b)
    z = jnp.zeros_like(w[:, 0])
    top = jnp.concatenate([w[:, 0], z], axis=2)
    bot = jnp.concatenate([z, w[:, 1]], axis=2)
    return jnp.concatenate([top, bot], axis=1)


def _uk_per_head(w_uk):
    h, a, b = w_uk.shape
    z = jnp.zeros_like(w_uk)
    even = jnp.concatenate([w_uk, z], axis=1)
    odd = jnp.concatenate([z, w_uk], axis=1)
    sel = (jnp.arange(h) % 2 == 0)[:, None, None]
    return jnp.where(sel, even, odd)


def kernel(x, meta_tokens, norm_mix_g, w_in, b_forget, q_norm_g, w_uq, kv_norm_g, w_uk, w_uv, w_qidx,
           kidx_ln_g, kidx_ln_b, w_branch_a, w_branch_b, w_out, norm_ffn_g, w_router, b_router, w_gate_up,
           b_gate_up, w_down, b_down, norm_final_g):
    batch, seq, d = x.shape
    t = seq + N_META
    tp = -(-t // Q_BLOCK) * Q_BLOCK
    k_sel = min(IDX_TOPK_MAX, seq // 4)
    depth = w_in.shape[0]
    h = jnp.concatenate([jnp.broadcast_to(meta_tokens.astype(x.dtype)[None], (batch, N_META, d)), x], axis=1)
    for l in range(depth):
        hp = jnp.pad(h, ((0, 0), (0, tp - t), (0, 0)))
        rows = batch * tp
        q, k, v, cq, small3, sga, sgb = _in_proj(
            hp.reshape(rows, d), norm_mix_g[l].astype(F32)[None], _layout_w_in(w_in[l]))
        fc = _forget_cumsum(small3, b_forget[l], batch, tp)
        fr = jnp.transpose(fc[:, :, FA_LO:FA_LO + FOX_HEADS], (0, 2, 1)).reshape(batch, FOX_HEADS // 2, 2, tp)
        sh = lambda a: a.reshape(batch, tp, a.shape[-1])
        kt = jnp.transpose(k.reshape(batch, tp, FOX_HEADS // 2, LANES), (0, 2, 3, 1))
        o_a = _fox_attention(sh(q), kt, sh(v), fr)
        qq, qidx, kk = _dsa_prep(cq, small3, tp, q_norm_g[l], kv_norm_g[l], kidx_ln_g[l], kidx_ln_b[l],
                                 _layout_w_q(w_uq[l], w_qidx[l]), _uk_per_head(w_uk[l]).astype(BF16))
        o_b = _dsa_attention(sh(qq), sh(qidx), sh(small3), sh(kk), _pair_blockdiag(w_uv[l]).astype(BF16), t, k_sel)
        w_r = jnp.pad(w_router[l], ((0, 0), (0, LANES - N_EXPERTS))).astype(BF16)
        b_r = jnp.pad(b_router[l].astype(F32), (0, LANES - N_EXPERTS), constant_values=NEG_INF)[None]
        h1, hn, ti, tg = _merge_router(
            o_a, o_b, sh(sga), sh(sgb), hp, t, w_branch_a[l].astype(BF16), w_branch_b[l].astype(BF16),
            w_out[l].astype(BF16), norm_ffn_g[l].astype(F32)[None], w_r, b_r)
        n_tok = batch * t
        block_expert, slot_tok, slot_dst = _routing_tables(ti[:, :, :TOP_K], n_tok)
        y = _experts(hn.reshape(n_tok * (d // LANES), LANES), block_expert, slot_tok, slot_dst,
                     w_gate_up[l].astype(BF16), b_gate_up[l].astype(F32)[:, None, :],
                     w_down[l].astype(BF16), b_down[l].astype(F32)[:, None, :])
        h = _combine(h1, y, tg, norm_final_g.astype(F32)[None], final_norm=(l == depth - 1))
    return h[:, N_META:]
```

```python
import functools

import numpy as np
import jax
import jax.numpy as jnp
from jax import lax
from jax.experimental import pallas as pl
from jax.experimental.pallas import tpu as pltpu

F32 = jnp.float32
BF16 = jnp.bfloat16
I32 = jnp.int32

D_MODEL = 1024
CHUNK = 64
N_META = 16
Q_BLOCK = 128
ROPE_THETA = 10000.0
NORM_EPS = 1e-6
NEG_INF = -1e30

FOX_HEADS = 8
FOX_HEAD_DIM = 64
DSA_HEADS = 8
Q_RANK = 256
KV_RANK = 128
QK_NOPE_DIM = 64
QK_ROPE_DIM = 32
V_HEAD_DIM = 64
IDX_HEADS = 4
IDX_HEAD_DIM = 64
IDX_ROPE_DIM = 32
IDX_TOPK_MAX = 256
N_EXPERTS = 32
TOP_K = 4
D_EXPERT = D_MODEL
SWIGLU_ALPHA = 1.702
SWIGLU_LIMIT = 7.0

FOX_WIDTH = FOX_HEADS * FOX_HEAD_DIM
DSA_WIDTH = DSA_HEADS * V_HEAD_DIM
IN_SPLITS = (FOX_WIDTH, FOX_WIDTH, FOX_WIDTH, FOX_HEADS, Q_RANK, KV_RANK, QK_ROPE_DIM,
             IDX_HEAD_DIM, IDX_HEADS, D_MODEL, D_MODEL)

LANES = 128
VMEM_LIMIT = 56 * 1024 * 1024
KEY_GROUP = 512
FOX_KEY_GROUP = 256
KEY_TILE = 256
Q_ROWS = 256
LOG2E = 1.4426950408889634
EXPERT_ROWS = 256
EXPERT_CHUNKS = 4

KR_LO, KIDX_LO = 0, 32
FA_LO, IW_LO = 64, 72

C_Q, C_K, C_V = 0, 512, 1024
C_CQ = 1536
C_G1, C_G2, C_G3 = 1792, 1920, 2048
C_GA, C_GB = 2176, 3200
C_TOTAL = 4224


def _cparams(sem):
    return pltpu.CompilerParams(dimension_semantics=sem, vmem_limit_bytes=VMEM_LIMIT)


def _row_tile(rows, cap):
    t = cap
    while rows % t:
        t //= 2
    return t


def _rot_cols(w):
    h = w.shape[-1] // 2
    return jnp.concatenate([-w[..., h:], w[..., :h]], axis=-1)


def _perm_cols(w):
    h = w.shape[-1] // 2
    return jnp.concatenate([w[..., h:], w[..., :h]], axis=-1)


def _layout_w_in(w_in):
    offs = np.cumsum((0,) + IN_SPLITS)
    qa, ka, va, fa, cq, ckv, kr, kidx, iw, ga, gb = [w_in[:, offs[i]:offs[i + 1]] for i in range(11)]
    d = w_in.shape[0]
    z = lambda n: jnp.zeros((d, n), w_in.dtype)
    g2 = jnp.concatenate([kr, kidx, z(32)], axis=1)
    g3 = jnp.concatenate([_rot_cols(kr), _perm_cols(kidx[:, :IDX_ROPE_DIM]), fa, iw, z(52)], axis=1)
    w = jnp.concatenate([qa, ka, va, cq, ckv, g2, g3, ga, gb], axis=1)
    assert w.shape[1] == C_TOTAL
    return w.astype(BF16)


def _layout_w_q(w_uq, w_qidx):
    r = w_uq.shape[0]
    wq = w_uq.reshape(r, DSA_HEADS, QK_NOPE_DIM + QK_ROPE_DIM)
    nope = wq[:, :, :QK_NOPE_DIM].reshape(r, DSA_HEADS * QK_NOPE_DIM)
    rope = wq[:, :, QK_NOPE_DIM:]
    pad_r = lambda a: jnp.pad(a, ((0, 0), (0, 0), (KR_LO, LANES - KR_LO - QK_ROPE_DIM))).reshape(r, -1)
    wi = w_qidx.reshape(r, IDX_HEADS, IDX_HEAD_DIM)
    wi_rot = jnp.concatenate([_rot_cols(wi[:, :, :IDX_ROPE_DIM]), jnp.zeros_like(wi[:, :, IDX_ROPE_DIM:])], axis=-1)
    pad_i = lambda a: jnp.pad(a, ((0, 0), (0, 0), (KIDX_LO, LANES - KIDX_LO - IDX_HEAD_DIM))).reshape(r, -1)
    w = jnp.concatenate([nope, pad_r(rope), pad_r(_rot_cols(rope)), pad_i(wi), pad_i(wi_rot)], axis=1)
    return w.astype(BF16)


def _rope_tables(tp):
    half = QK_ROPE_DIM // 2
    inv = ROPE_THETA ** (-jnp.arange(half, dtype=F32) / half)
    ang = jnp.arange(tp, dtype=F32)[:, None] * inv[None, :]
    cos, sin = jnp.cos(ang), jnp.sin(ang)
    cos32 = jnp.concatenate([cos, cos], axis=1)
    sin32 = jnp.concatenate([sin, sin], axis=1)
    one, zero = jnp.ones((tp, 1), F32), jnp.zeros((tp, 1), F32)
    cos_k = jnp.concatenate([cos32, cos32, jnp.tile(one, (1, 64))], axis=1)
    sin_k = jnp.concatenate([sin32, sin32, jnp.tile(zero, (1, 64))], axis=1)
    return cos_k, sin_k


def _inproj_kernel(x_ref, g_ref, w_ref, q_ref, k_ref, v_ref, cq_ref, sm_ref, ga_ref, gb_ref):
    x = x_ref[...]
    ms = jnp.mean(x * x, axis=-1, keepdims=True)
    xn = (x * lax.rsqrt(ms + NORM_EPS) * g_ref[...]).astype(BF16)

    def mm(lo, hi):
        return jnp.dot(xn, w_ref[:, lo:hi], preferred_element_type=F32)

    q_ref[...] = (mm(C_Q, C_K) * (FOX_HEAD_DIM ** -0.5 * LOG2E)).astype(BF16)
    k_ref[...] = mm(C_K, C_V).astype(BF16)
    v_ref[...] = mm(C_V, C_CQ).astype(BF16)
    cq_ref[...] = mm(C_CQ, C_G1)
    sm_ref[...] = mm(C_G1, C_GA)
    ga_ref[...] = (1.0 / (1.0 + jnp.exp(-mm(C_GA, C_GB)))).astype(BF16)
    gb_ref[...] = (1.0 / (1.0 + jnp.exp(-mm(C_GB, C_TOTAL)))).astype(BF16)


def _in_proj(hp, g, w_cat):
    rows, d = hp.shape
    tm = _row_tile(rows, 512)
    row = lambda n: pl.BlockSpec((tm, n), lambda i: (i, 0))
    full = lambda a: pl.BlockSpec(a.shape, lambda i: (0,) * a.ndim)
    outs = [(FOX_WIDTH, BF16)] * 3 + [(Q_RANK, F32), (3 * LANES, F32), (D_MODEL, BF16), (D_MODEL, BF16)]
    return pl.pallas_call(
        _inproj_kernel,
        grid=(rows // tm,),
        in_specs=[row(d), full(g), full(w_cat)],
        out_specs=[row(n) for n, _ in outs],
        out_shape=[jax.ShapeDtypeStruct((rows, n), dt) for n, dt in outs],
        compiler_params=_cparams(("parallel",)),
        name="in_proj",
    )(hp, g, w_cat)


def _split3(x):
    hi = x.astype(BF16)
    r1 = x - hi.astype(F32)
    mid = r1.astype(BF16)
    lo = (r1 - mid.astype(F32)).astype(BF16)
    return hi, mid, lo


def _fcum_kernel(sm_ref, b_ref, tri_ref, o_ref):
    tp = o_ref.shape[1]
    tri = tri_ref[...]
    carry = jnp.zeros((1, LANES), F32)
    for j in range(tp // LANES):
        z = sm_ref[0, j * LANES:(j + 1) * LANES, :] + b_ref[...]
        ls = jnp.minimum(z, 0.0) - jnp.log(1.0 + jnp.exp(-jnp.abs(z)))
        c = carry
        for part in _split3(ls):
            c = c + jnp.dot(tri, part, preferred_element_type=F32)
        o_ref[0, j * LANES:(j + 1) * LANES, :] = c * LOG2E
        carry = c[LANES - 1:LANES, :]


def _forget_cumsum(small3, b_forget, batch, tp):
    g3 = small3.reshape(batch, tp, 3 * LANES)
    bias = jnp.zeros((1, LANES), F32).at[0, FA_LO:FA_LO + FOX_HEADS].set(b_forget.astype(F32))
    tri = jnp.tril(jnp.ones((LANES, LANES), F32)).astype(BF16)
    return pl.pallas_call(
        _fcum_kernel,
        grid=(batch,),
        in_specs=[pl.BlockSpec((1, tp, LANES), lambda b: (b, 0, 2)),
                  pl.BlockSpec((1, LANES), lambda b: (0, 0)),
                  pl.BlockSpec((LANES, LANES), lambda b: (0, 0))],
        out_specs=pl.BlockSpec((1, tp, LANES), lambda b: (b, 0, 0)),
        out_shape=jax.ShapeDtypeStruct((batch, tp, LANES), F32),
        compiler_params=_cparams(("parallel",)),
        name="fcum",
    )(g3, bias, tri)


def _key_widths(tp, group):
    ws = list(range(group, tp, group)) + [tp]
    return ws


def _query_steps(tp):
    return -(-tp // Q_ROWS)


def _for_key_width(qi, tp, body, q_rows=Q_ROWS, lookahead=0, group=KEY_GROUP):
    groups = []
    for step in range(-(-tp // q_rows)):
        rows = min(q_rows, tp - step * q_rows)
        need = min(tp, step * q_rows + rows + lookahead)
        w = min(x for x in _key_widths(tp, group) if x >= need)
        if groups and groups[-1][:2] == [rows, w]:
            groups[-1][3] = step
        else:
            groups.append([rows, w, step, step])
    for rows, w, first, last in groups:
        pl.when((qi >= first) & (qi <= last))(functools.partial(body, rows, w, first))


def _key_tiles(w):
    return [(k0, min(KEY_TILE, w - k0)) for k0 in range(0, w, KEY_TILE)]


def _lane_group_max(mx, s):
    for c in range(s.shape[1] // LANES):
        mx = jnp.maximum(mx, s[:, c * LANES:(c + 1) * LANES])
    return mx


def _fox_kernel(q_ref, kt_ref, v_ref, fr_ref, o_ref, s_ref):
    qi = pl.program_id(2)
    tp = kt_ref.shape[3]

    def body(rows, w, first):
        q = q_ref[0, :rows]
        lane = lax.broadcasted_iota(I32, (rows, LANES), 1)
        zero = jnp.zeros_like(q)
        qs = jnp.concatenate([jnp.where(lane < FOX_HEAD_DIM, q, zero),
                              jnp.where(lane >= FOX_HEAD_DIM, q, zero)], axis=0)
        mxs = [jnp.full((rows, LANES), -jnp.inf, F32)] * 2
        for k0, tk in _key_tiles(w):
            s2 = jnp.dot(qs, kt_ref[0, 0, :, k0:k0 + tk], preferred_element_type=F32)
            for hh in range(2):
                s = s2[hh * rows:(hh + 1) * rows] - fr_ref[0, 0, hh:hh + 1, k0:k0 + tk]
                if k0 + tk > first * Q_ROWS:
                    qpos = qi * Q_ROWS + lax.broadcasted_iota(I32, (rows, tk), 0)
                    kpos = k0 + lax.broadcasted_iota(I32, (rows, tk), 1)
                    s = jnp.where(kpos <= qpos, s, NEG_INF)
                s_ref[hh, :rows, k0:k0 + tk] = s
                mxs[hh] = _lane_group_max(mxs[hh], s)
        ms = [jnp.max(mx, axis=-1, keepdims=True) for mx in mxs]
        acc = jnp.zeros((2 * rows, 2 * LANES), F32)
        for k0, tk in _key_tiles(w):
            p2 = jnp.concatenate([jnp.exp2((s_ref[hh, :rows, k0:k0 + tk] - ms[hh]).astype(BF16))
                                  for hh in range(2)], axis=0)
            ones_lane = jnp.where(lax.broadcasted_iota(I32, (tk, LANES), 1) == 0, 1.0, 0.0).astype(BF16)
            v_aug = jnp.concatenate([v_ref[0, k0:k0 + tk, :], ones_lane], axis=1)
            acc = acc + jnp.dot(p2, v_aug, preferred_element_type=F32)
        o = acc[:, :LANES] * (1.0 / acc[:, LANES:LANES + 1])
        o_ref[0, :rows] = jnp.where(lane < FOX_HEAD_DIM, o[:rows], o[rows:]).astype(BF16)

    _for_key_width(qi, tp, body, group=FOX_KEY_GROUP)


def _fox_attention(q, kt, v, fr):
    batch, tp, _ = q.shape
    nq = _query_steps(tp)
    pairs = FOX_HEADS // 2
    return pl.pallas_call(
        _fox_kernel,
        grid=(batch, pairs, nq),
        in_specs=[pl.BlockSpec((1, Q_ROWS, LANES), lambda b, p, i: (b, i, p)),
                  pl.BlockSpec((1, 1, LANES, tp), lambda b, p, i: (b, p, 0, 0)),
                  pl.BlockSpec((1, tp, LANES), lambda b, p, i: (b, 0, p)),
                  pl.BlockSpec((1, 1, 2, tp), lambda b, p, i: (b, p, 0, 0))],
        out_specs=pl.BlockSpec((1, Q_ROWS, LANES), lambda b, p, i: (b, i, p)),
        out_shape=jax.ShapeDtypeStruct((batch, tp, FOX_WIDTH), BF16),
        scratch_shapes=[pltpu.VMEM((2, Q_ROWS, tp), F32)],
        compiler_params=_cparams(("parallel", "parallel", "arbitrary")),
        name="fox",
    )(q, kt, v, fr)


def _dsa_prep_kernel(cq_ref, sm_ref, qg_ref, kvg_ref, lng_ref, lnb_ref, lngp_ref, lnbp_ref,
                     cosk_ref, sink_ref, wq_ref, wuk_ref, qq_ref, qidx_ref, kk_ref):
    cq = cq_ref[...]
    ms = jnp.mean(cq * cq, axis=-1, keepdims=True)
    cqn = (cq * lax.rsqrt(ms + NORM_EPS) * qg_ref[...]).astype(BF16)
    cos = cosk_ref[...]
    sin = sink_ref[...]

    def mm(lo, hi):
        return jnp.dot(cqn, wq_ref[:, lo:hi], preferred_element_type=F32)

    nope_w = DSA_HEADS * QK_NOPE_DIM
    hw = DSA_HEADS * LANES
    iw = IDX_HEADS * LANES
    q_nope = mm(0, nope_w).astype(BF16)
    for h in range(DSA_HEADS):
        base = nope_w + h * LANES
        q_rope = mm(base, base + LANES) * cos + mm(base + hw, base + hw + LANES) * sin
        p = h // 2
        q_lat = jnp.dot(q_nope[:, p * LANES:(p + 1) * LANES], wuk_ref[h], preferred_element_type=F32)
        qscale = (QK_NOPE_DIM + QK_ROPE_DIM) ** -0.5 * LOG2E
        qq_ref[:, h * 2 * LANES:h * 2 * LANES + LANES] = (q_lat * qscale).astype(BF16)
        qq_ref[:, h * 2 * LANES + LANES:(h + 1) * 2 * LANES] = (q_rope * qscale).astype(BF16)
    for h in range(IDX_HEADS):
        base = nope_w + 2 * hw + h * LANES
        qi = mm(base, base + LANES) * cos + mm(base + iw, base + iw + LANES) * sin
        qidx_ref[:, h * LANES:(h + 1) * LANES] = qi.astype(BF16)

    ckv = sm_ref[:, 0:LANES]
    ms = jnp.mean(ckv * ckv, axis=-1, keepdims=True)
    kk_ref[:, 0:LANES] = (ckv * lax.rsqrt(ms + NORM_EPS) * kvg_ref[...]).astype(BF16)
    g2 = sm_ref[:, LANES:2 * LANES]
    g3 = sm_ref[:, 2 * LANES:3 * LANES]
    lane = lax.broadcasted_iota(I32, g2.shape, 1)
    is_idx = (lane >= KIDX_LO) & (lane < KIDX_LO + IDX_HEAD_DIM)
    is_rope = lane < KIDX_LO
    xi = jnp.where(is_idx, g2, 0.0)
    mu = jnp.sum(xi, axis=-1, keepdims=True) * (1.0 / IDX_HEAD_DIM)
    xc = jnp.where(is_idx, g2 - mu, 0.0)
    var = jnp.sum(xc * xc, axis=-1, keepdims=True) * (1.0 / IDX_HEAD_DIM)
    rstd = lax.rsqrt(var + NORM_EPS)
    y = xc * rstd * lng_ref[...] + lnb_ref[...]
    yp = (g3 - mu) * rstd * lngp_ref[...] + lnbp_ref[...]
    a = jnp.where(is_rope, g2, y)
    b = jnp.where(is_rope, g3, jnp.where(lane < KIDX_LO + IDX_ROPE_DIM, yp, 0.0))
    kx = jnp.where(lane == LANES - 1, 1.0, a * cos + b * sin)
    kk_ref[:, LANES:2 * LANES] = kx.astype(BF16)


def _dsa_prep(cq, small3, tp, q_norm_g, kv_norm_g, ln_g, ln_b, w_q, w_uk_pair):
    rows = cq.shape[0]
    tm = tp // 4 if tp % 64 == 0 else tp
    per = tp // tm
    cos_k, sin_k = _rope_tables(tp)
    lane_vec = lambda v, lo: jnp.zeros((1, LANES), F32).at[0, lo:lo + v.shape[0]].set(v.astype(F32))
    half = IDX_ROPE_DIM // 2
    sign = jnp.concatenate([-jnp.ones((half,), F32), jnp.ones((half,), F32)])
    g_perm = jnp.concatenate([ln_g[half:IDX_ROPE_DIM], ln_g[:half]]).astype(F32) * sign
    b_perm = jnp.concatenate([ln_b[half:IDX_ROPE_DIM], ln_b[:half]]).astype(F32) * sign
    row = lambda n: pl.BlockSpec((tm, n), lambda i: (i, 0))
    tab = pl.BlockSpec((tm, LANES), lambda i: (i % per, 0))
    full = lambda a: pl.BlockSpec(a.shape, lambda i: (0,) * a.ndim)
    vecs = [q_norm_g.astype(F32)[None], kv_norm_g.astype(F32)[None], lane_vec(ln_g, KIDX_LO),
            lane_vec(ln_b, KIDX_LO), lane_vec(g_perm, KIDX_LO), lane_vec(b_perm, KIDX_LO)]
    outs = [(DSA_HEADS * 2 * LANES, BF16), (IDX_HEADS * LANES, BF16), (2 * LANES, BF16)]
    return pl.pallas_call(
        _dsa_prep_kernel,
        grid=(rows // tm,),
        in_specs=[row(Q_RANK), row(3 * LANES)] + [full(v) for v in vecs] + [tab, tab, full(w_q), full(w_uk_pair)],
        out_specs=[row(n) for n, _ in outs],
        out_shape=[jax.ShapeDtypeStruct((rows, n), dt) for n, dt in outs],
        compiler_params=_cparams(("parallel",)),
        name="dsa_prep",
    )(cq, small3, *vecs, cos_k, sin_k, w_q, w_uk_pair)


def _dsa_kernel(qq_ref, qidx_ref, sm_ref, kk_ref, kkt_ref, tri_ref, wuv_ref, o_ref, key_ref, bias_ref, *,
                n_valid, k_sel):
    qi = pl.program_id(1)
    tp = kk_ref.shape[1]
    int_min = jnp.int32(-2 ** 31)

    def body(rows, w, first):
        del first
        kxt = kkt_ref[0, LANES:, :w]
        acc = jnp.zeros((rows, w), F32)
        for h in range(IDX_HEADS):
            d = jnp.dot(qidx_ref[0, :, h * LANES:(h + 1) * LANES], kxt, preferred_element_type=F32)
            wh = sm_ref[0, :, IW_LO + h:IW_LO + h + 1] * (IDX_HEADS ** -0.5)
            acc = acc + wh * jnp.maximum(d, 0.0)
        isc = acc * (IDX_HEAD_DIM ** -0.5)
        isc = jnp.where(isc == 0.0, 0.0, isc)
        qpos = qi * rows + lax.broadcasted_iota(I32, (rows, 1), 0)
        kend = (((qpos + (CHUNK - N_META)) // CHUNK) + 1) * CHUNK - (CHUNK - N_META)
        kend = jnp.minimum(kend, n_valid)
        kpos = lax.broadcasted_iota(I32, (rows, w), 1)
        adm = kpos < kend
        bits = pltpu.bitcast(isc, I32)
        key = bits ^ ((bits >> 31) & jnp.int32(0x7FFFFFFF))
        key = jnp.where(adm, key, int_min)
        key_ref[:, :w] = key
        ok_bias = jnp.where(adm & (isc > 0.5 * NEG_INF), 0.0, NEG_INF)

        def count_ge(c):
            return jnp.sum(jnp.where(key_ref[:, :w] >= c, 1.0, 0.0), axis=-1, keepdims=True)

        kf = jnp.float32(k_sel)
        t0 = jnp.where(count_ge(jnp.zeros((rows, 1), I32)) >= kf, jnp.int32(0), int_min)

        def bit_step(i, t):
            cand = t | jnp.left_shift(jnp.int32(1), 30 - i)
            return jnp.where(count_ge(cand) >= kf, cand, t)

        thr = lax.fori_loop(0, 31, bit_step, t0)

        gt = key > thr
        need = kf - jnp.sum(jnp.where(gt, 1.0, 0.0), axis=-1, keepdims=True)
        run = jnp.zeros((rows, 1), F32)
        tri = tri_ref[...]
        for j in range(w // LANES):
            sl = slice(j * LANES, (j + 1) * LANES)
            eq = key[:, sl] == thr
            e = jnp.where(eq, 1.0, 0.0)
            pre = jnp.dot(e.astype(BF16), tri, preferred_element_type=F32) + run
            take = gt[:, sl] | (eq & (pre <= need))
            bias_ref[:, sl] = jnp.where(take, ok_bias[:, sl], NEG_INF)
            run = pre[:, LANES - 1:LANES]

        bias = bias_ref[:, :w]
        kkt = kkt_ref[0, :, :w]
        kv = kk_ref[0, :w, :LANES]
        for p in range(DSA_HEADS // 2):
            lat = []
            for h in (2 * p, 2 * p + 1):
                qh = qq_ref[0, :, h * 2 * LANES:(h + 1) * 2 * LANES]
                s = jnp.dot(qh, kkt, preferred_element_type=F32) + bias
                m = jnp.max(s, axis=-1, keepdims=True)
                pr = jnp.exp2(s - m)
                l = jnp.sum(pr, axis=-1, keepdims=True)
                lat.append((jnp.dot(pr.astype(BF16), kv, preferred_element_type=F32) * (1.0 / l)).astype(BF16))
            o_lat = jnp.concatenate(lat, axis=1)
            o_ref[0, :, p * LANES:(p + 1) * LANES] = jnp.dot(
                o_lat, wuv_ref[p], preferred_element_type=F32).astype(BF16)

    _for_key_width(qi, tp, body, q_rows=Q_BLOCK, lookahead=Q_BLOCK)


def _dsa_attention(qq, qidx, small3, kk, w_uv_pair, n_valid, k_sel):
    batch, tp, _ = kk.shape
    nq = tp // Q_BLOCK
    tri = jnp.triu(jnp.ones((LANES, LANES), F32)).astype(BF16)
    qblk = lambda n: pl.BlockSpec((1, Q_BLOCK, n), lambda b, i: (b, i, 0))
    return pl.pallas_call(
        functools.partial(_dsa_kernel, n_valid=n_valid, k_sel=k_sel),
        grid=(batch, nq),
        in_specs=[qblk(DSA_HEADS * 2 * LANES), qblk(IDX_HEADS * LANES),
                  pl.BlockSpec((1, Q_BLOCK, LANES), lambda b, i: (b, i, 2)),
                  pl.BlockSpec((1, tp, 2 * LANES), lambda b, i: (b, 0, 0)),
                  pl.BlockSpec((1, 2 * LANES, tp), lambda b, i: (b, 0, 0)),
                  pl.BlockSpec((LANES, LANES), lambda b, i: (0, 0)),
                  pl.BlockSpec(w_uv_pair.shape, lambda b, i: (0, 0, 0))],
        out_specs=qblk(DSA_WIDTH),
        out_shape=jax.ShapeDtypeStruct((batch, tp, DSA_WIDTH), BF16),
        scratch_shapes=[pltpu.VMEM((Q_BLOCK, tp), I32), pltpu.VMEM((Q_BLOCK, tp), F32)],
        compiler_params=_cparams(("parallel", "arbitrary")),
        name="dsa",
    )(qq, qidx, small3, kk, jnp.transpose(kk, (0, 2, 1)), tri, w_uv_pair)


def _store_row_tiles(ref, x):
    rows, d = x.shape
    sub = d // LANES
    for j in range(sub):
        ref[pl.ds(j, rows, stride=sub), :] = x[:, j * LANES:(j + 1) * LANES]


def _load_row_tiles(ref, rows):
    sub = ref.shape[0] // rows
    return jnp.concatenate([ref[pl.ds(j, rows, stride=sub), :] for j in range(sub)], axis=1)


def _merge_kernel(oa_ref, ob_ref, ga_ref, gb_ref, h_ref, wa_ref, wb_ref, wo_ref, g_ref, wr_ref, br_ref,
                  h1_ref, hn_ref, ti_ref, tg_ref):
    a = jnp.dot(oa_ref[0], wa_ref[...], preferred_element_type=F32)
    b = jnp.dot(ob_ref[0], wb_ref[...], preferred_element_type=F32)
    merged = ga_ref[0].astype(F32) * a + gb_ref[0].astype(F32) * b
    mix = jnp.dot(merged.astype(BF16), wo_ref[...], preferred_element_type=F32)
    h1 = h_ref[0] + mix
    h1_ref[0] = h1
    ms = jnp.mean(h1 * h1, axis=-1, keepdims=True)
    hn = h1 * lax.rsqrt(ms + NORM_EPS) * g_ref[...]
    _store_row_tiles(hn_ref.at[0], hn)
    logits = jnp.dot(hn.astype(BF16), wr_ref[...], preferred_element_type=F32) + br_ref[...]
    lane = lax.broadcasted_iota(I32, logits.shape, 1)
    idx_out = jnp.zeros(logits.shape, I32)
    val_out = jnp.zeros(logits.shape, F32)
    vals = []
    for kk in range(TOP_K):
        m = jnp.max(logits, axis=-1, keepdims=True)
        idx = jnp.min(jnp.where(logits == m, lane, LANES), axis=-1, keepdims=True)
        logits = jnp.where(lane == idx, -jnp.inf, logits)
        idx_out = jnp.where(lane == kk, idx, idx_out)
        vals.append(m)
    es = [jnp.exp(v - vals[0]) for v in vals]
    den = es[0] + es[1] + es[2] + es[3]
    for kk in range(TOP_K):
        val_out = jnp.where(lane == kk, es[kk] / den, val_out)
    ti_ref[0] = idx_out
    tg_ref[0] = val_out


def _merge_router(o_a, o_b, sga, sgb, hp, t_valid, w_a, w_b, w_o, g_ffn, w_r, b_r):
    batch, tp, d = hp.shape
    tt = t_valid // 3 if t_valid % 48 == 0 else t_valid
    blk = lambda n: pl.BlockSpec((1, tt, n), lambda b, j: (b, j, 0))
    full = lambda a: pl.BlockSpec(a.shape, lambda b, j: (0,) * a.ndim)
    consts = [w_a, w_b, w_o, g_ffn, w_r, b_r]
    sub = d // LANES
    return pl.pallas_call(
        _merge_kernel,
        grid=(batch, t_valid // tt),
        in_specs=[blk(FOX_WIDTH), blk(DSA_WIDTH), blk(d), blk(d), blk(d)] + [full(c) for c in consts],
        out_specs=[blk(d), pl.BlockSpec((1, tt * sub, LANES), lambda b, j: (b, j, 0)), blk(LANES), blk(LANES)],
        out_shape=[jax.ShapeDtypeStruct((batch, t_valid, d), F32),
                   jax.ShapeDtypeStruct((batch, t_valid * sub, LANES), F32),
                   jax.ShapeDtypeStruct((batch, t_valid, LANES), I32),
                   jax.ShapeDtypeStruct((batch, t_valid, LANES), F32)],
        compiler_params=_cparams(("parallel", "parallel")),
        name="merge_router",
    )(o_a, o_b, sga, sgb, hp, *consts)


def _experts_kernel(be_ref, tok_ref, tokn_ref, dstp_ref, dst_ref, hn_hbm, wgu_ref, bgu_ref, wd_ref, bd_ref,
                    y_hbm, x0, x1, y0, y1, xb_ref, act_ref, gsem, ssem):
    i = pl.program_id(0)
    nb = pl.num_programs(0)
    sub = wd_ref.shape[2] // LANES
    rows = x0.shape[0] // sub
    xbufs, ybufs = (x0, x1), (y0, y1)

    def gather_row(idx_ref, s, r):
        src = pl.multiple_of(idx_ref[0, 0, r], sub)
        return pltpu.make_async_copy(hn_hbm.at[pl.ds(src, sub)], xbufs[s].at[pl.ds(r * sub, sub)], gsem.at[s])

    def scatter_row(idx_ref, s, r):
        dst = pl.multiple_of(idx_ref[0, 0, r], sub)
        return pltpu.make_async_copy(ybufs[s].at[pl.ds(r * sub, sub)], y_hbm.at[pl.ds(dst, sub)], ssem.at[s])

    def wait_gather(s):
        pltpu.make_async_copy(hn_hbm.at[pl.ds(0, rows * sub)], xbufs[s], gsem.at[s]).wait()

    def wait_scatter(s):
        pltpu.make_async_copy(ybufs[s], y_hbm.at[pl.ds(0, rows * sub)], ssem.at[s]).wait()

    @pl.when(i == 0)
    def _():
        y1[...] = jnp.zeros(y1.shape, y1.dtype)

        def issue(r, c):
            gather_row(tok_ref, 0, r).start()
            return c
        lax.fori_loop(0, rows, issue, 0)

    def step(slot):
        other = 1 - slot

        @pl.when(i >= 1)
        def _():
            wait_scatter(slot)

        wait_gather(slot)
        xb_ref[...] = _load_row_tiles(xbufs[slot], rows).astype(BF16)
        per = rows // EXPERT_CHUNKS
        cw = D_EXPERT // EXPERT_CHUNKS
        for c in range(EXPERT_CHUNKS):
            @pl.when(be_ref[i] >= 0)
            def _():
                for r in range(c * per, (c + 1) * per):
                    gather_row(tokn_ref, other, r).start()
                for r in range(c * per, (c + 1) * per):
                    scatter_row(dstp_ref, other, r).start()
                x = xb_ref[...]
                glu = jnp.dot(x, wgu_ref[0, :, c * cw:(c + 1) * cw], preferred_element_type=F32)
                glu = jnp.minimum(glu + bgu_ref[0, :, c * cw:(c + 1) * cw], SWIGLU_LIMIT)
                lo = D_EXPERT + c * cw
                lin = jnp.dot(x, wgu_ref[0, :, lo:lo + cw], preferred_element_type=F32)
                lin = jnp.clip(lin + bgu_ref[0, :, lo:lo + cw], -SWIGLU_LIMIT, SWIGLU_LIMIT)
                act = glu * (1.0 / (1.0 + jnp.exp(-SWIGLU_ALPHA * glu))) * (lin + 1.0)
                act_ref[:, c * cw:(c + 1) * cw] = act.astype(BF16)

        _store_row_tiles(ybufs[slot],
                         jnp.dot(act_ref[...], wd_ref[0], preferred_element_type=F32) + bd_ref[0])

        @pl.when(i == nb - 1)
        def _():
            def issue(r, c):
                scatter_row(dst_ref, slot, r).start()
                return c
            lax.fori_loop(0, rows, issue, 0)
            wait_scatter(slot)
            wait_scatter(other)
            wait_gather(other)

    pl.when(i % 2 == 0)(functools.partial(step, 0))
    pl.when(i % 2 == 1)(functools.partial(step, 1))


def _experts(hn, block_expert, slot_tok, slot_dst, w_gu, b_gu, w_d, b_d):
    d = w_d.shape[2]
    sub = d // LANES
    n_slots = slot_tok.shape[0]
    rows = EXPERT_ROWS
    nb = n_slots // rows
    tok3 = (slot_tok * sub).reshape(nb, 1, rows)
    dst3 = (slot_dst * sub).reshape(nb, 1, rows)
    smem = lambda f: pl.BlockSpec((1, 1, rows), f, memory_space=pltpu.SMEM)
    grid_spec = pltpu.PrefetchScalarGridSpec(
        num_scalar_prefetch=1,
        grid=(nb,),
        in_specs=[smem(lambda i, be: (i, 0, 0)),
                  smem(lambda i, be: (jnp.minimum(i + 1, nb - 1), 0, 0)),
                  smem(lambda i, be: (jnp.maximum(i - 1, 0), 0, 0)),
                  smem(lambda i, be: (i, 0, 0)),
                  pl.BlockSpec(memory_space=pl.ANY),
                  pl.BlockSpec((1, d, 2 * D_EXPERT), lambda i, be: (be[i], 0, 0)),
                  pl.BlockSpec((1, 1, 2 * D_EXPERT), lambda i, be: (be[i], 0, 0)),
                  pl.BlockSpec((1, D_EXPERT, d), lambda i, be: (be[i], 0, 0)),
                  pl.BlockSpec((1, 1, d), lambda i, be: (be[i], 0, 0))],
        out_specs=pl.BlockSpec(memory_space=pl.ANY),
        scratch_shapes=[pltpu.VMEM((rows * sub, LANES), F32)] * 4 +
                       [pltpu.VMEM((rows, d), BF16), pltpu.VMEM((rows, D_EXPERT), BF16)] +
                       [pltpu.SemaphoreType.DMA((2,)), pltpu.SemaphoreType.DMA((2,))],
    )
    return pl.pallas_call(
        _experts_kernel,
        grid_spec=grid_spec,
        out_shape=jax.ShapeDtypeStruct((n_slots * sub, LANES), F32),
        compiler_params=_cparams(("arbitrary",)),
        name="experts",
    )(block_expert, tok3, tok3, dst3, dst3, hn, w_gu, b_gu, w_d, b_d)


def _routing_tables(top_idx, n_tok):
    rows = EXPERT_ROWS
    n_asg = n_tok * TOP_K
    flat_e = top_idx.reshape(-1).astype(I32)
    keys = jnp.sort(flat_e * n_asg + jnp.arange(n_asg, dtype=I32))
    order = keys % n_asg
    counts = jnp.sum(flat_e[:, None] == jnp.arange(N_EXPERTS, dtype=I32)[None, :], axis=0).astype(I32)
    padded = (counts + rows - 1) // rows * rows
    start = jnp.cumsum(counts) - counts
    pend = jnp.cumsum(padded)
    pstart = pend - padded
    nb = -(-n_asg // rows) + N_EXPERTS
    n_slots = nb * rows
    block_start = jnp.arange(nb, dtype=I32) * rows
    block_expert = jnp.minimum(
        jnp.sum((pend[None, :] <= block_start[:, None]).astype(I32), axis=1), N_EXPERTS - 1).astype(I32)
    slot = jnp.arange(n_slots, dtype=I32)
    e = jnp.repeat(block_expert, rows)
    r = slot - pstart[e]
    valid = (r >= 0) & (r < counts[e])
    asg = order[jnp.clip(start[e] + r, 0, n_asg - 1)]
    slot_tok = jnp.where(valid, asg // TOP_K, 0).astype(I32)
    pad_rank = jnp.cumsum(jnp.logical_not(valid).astype(I32)) - 1
    slot_dst = jnp.where(valid, (asg % TOP_K) * n_tok + asg // TOP_K, n_asg + pad_rank).astype(I32)
    return block_expert, slot_tok, slot_dst


def _combine_kernel(h_ref, y0_ref, y1_ref, y2_ref, y3_ref, tg_ref, g_ref, o_ref, *, final_norm):
    h = h_ref[0]
    for kk, y_ref in enumerate((y0_ref, y1_ref, y2_ref, y3_ref)):
        h = h + _load_row_tiles(y_ref, h.shape[0]) * tg_ref[0, :, kk:kk + 1]
    if final_norm:
        ms = jnp.mean(h * h, axis=-1, keepdims=True)
        h = h * lax.rsqrt(ms + NORM_EPS) * g_ref[...]
    o_ref[0] = h


def _combine(h1, y, tg, g, final_norm):
    batch, t, d = h1.shape
    tt = t // 6 if t % 48 == 0 else t
    per_batch = t // tt
    per_pick = batch * per_batch
    blk = lambda n: pl.BlockSpec((1, tt, n), lambda b, j: (b, j, 0))
    pick = lambda kk: pl.BlockSpec((tt * (d // LANES), LANES), lambda b, j: (kk * per_pick + b * per_batch + j, 0))
    return pl.pallas_call(
        functools.partial(_combine_kernel, final_norm=final_norm),
        grid=(batch, per_batch),
        in_specs=[blk(d)] + [pick(kk) for kk in range(TOP_K)] + [blk(LANES), pl.BlockSpec((1, d), lambda b, j: (0, 0))],
        out_specs=blk(d),
        out_shape=jax.ShapeDtypeStruct((batch, t, d), F32),
        compiler_params=_cparams(("parallel", "parallel")),
        name="combine",
    )(h1, y, y, y, y, tg, g)


def _pair_blockdiag(w):
    h, a, b = w.shape
    w = w.reshape(h // 2, 2, a, b)
    z = jnp.zeros_like(w[:, 0])
    top = jnp.concatenate([w[:, 0], z], axis=2)
    bot = jnp.concatenate([z, w[:, 1]], axis=2)
    return jnp.concatenate([top, bot], axis=1)


def _uk_per_head(w_uk):
    h, a, b = w_uk.shape
    z = jnp.zeros_like(w_uk)
    even = jnp.concatenate([w_uk, z], axis=1)
    odd = jnp.concatenate([z, w_uk], axis=1)
    sel = (jnp.arange(h) % 2 == 0)[:, None, None]
    return jnp.where(sel, even, odd)


def kernel(x, meta_tokens, norm_mix_g, w_in, b_forget, q_norm_g, w_uq, kv_norm_g, w_uk, w_uv, w_qidx,
           kidx_ln_g, kidx_ln_b, w_branch_a, w_branch_b, w_out, norm_ffn_g, w_router, b_router, w_gate_up,
           b_gate_up, w_down, b_down, norm_final_g):
    batch, seq, d = x.shape
    t = seq + N_META
    tp = -(-t // Q_BLOCK) * Q_BLOCK
    k_sel = min(IDX_TOPK_MAX, seq // 4)
    depth = w_in.shape[0]
    h = jnp.concatenate([jnp.broadcast_to(meta_tokens.astype(x.dtype)[None], (batch, N_META, d)), x], axis=1)
    for l in range(depth):
        hp = jnp.pad(h, ((0, 0), (0, tp - t), (0, 0)))
        rows = batch * tp
        q, k, v, cq, small3, sga, sgb = _in_proj(
            hp.reshape(rows, d), norm_mix_g[l].astype(F32)[None], _layout_w_in(w_in[l]))
        fc = _forget_cumsum(small3, b_forget[l], batch, tp)
        fr = jnp.transpose(fc[:, :, FA_LO:FA_LO + FOX_HEADS], (0, 2, 1)).reshape(batch, FOX_HEADS // 2, 2, tp)
        sh = lambda a: a.reshape(batch, tp, a.shape[-1])
        kt = jnp.transpose(k.reshape(batch, tp, FOX_HEADS // 2, LANES), (0, 2, 3, 1))
        o_a = _fox_attention(sh(q), kt, sh(v), fr)
        qq, qidx, kk = _dsa_prep(cq, small3, tp, q_norm_g[l], kv_norm_g[l], kidx_ln_g[l], kidx_ln_b[l],
                                 _layout_w_q(w_uq[l], w_qidx[l]), _uk_per_head(w_uk[l]).astype(BF16))
        o_b = _dsa_attention(sh(qq), sh(qidx), sh(small3), sh(kk), _pair_blockdiag(w_uv[l]).astype(BF16), t, k_sel)
        w_r = jnp.pad(w_router[l], ((0, 0), (0, LANES - N_EXPERTS))).astype(BF16)
        b_r = jnp.pad(b_router[l].astype(F32), (0, LANES - N_EXPERTS), constant_values=NEG_INF)[None]
        h1, hn, ti, tg = _merge_router(
            o_a, o_b, sh(sga), sh(sgb), hp, t, w_branch_a[l].astype(BF16), w_branch_b[l].astype(BF16),
            w_out[l].astype(BF16), norm_ffn_g[l].astype(F32)[None], w_r, b_r)
        n_tok = batch * t
        block_expert, slot_tok, slot_dst = _routing_tables(ti[:, :, :TOP_K], n_tok)
        y = _experts(hn.reshape(n_tok * (d // LANES), LANES), block_expert, slot_tok, slot_dst,
                     w_gate_up[l].astype(BF16), b_gate_up[l].astype(F32)[:, None, :],
                     w_down[l].astype(BF16), b_down[l].astype(F32)[:, None, :])
        h = _combine(h1, y, tg, norm_final_g.astype(F32)[None], final_norm=(l == depth - 1))
    return h[:, N_META:]
```
